```python
import math
import jax, jax.numpy as jnp
from jax import lax
import numpy as np

D_MODEL = 1024
BATCH = 4
SEQ = 4096
DEPTH = 2
DEC_BATCH = 16
DEC_SEQ = 4096
PAST_LEN = 128

GRID_W = 64
BLOCK = 128
EPS = 1e-6
NEG_INF = -1e30
ROPE_THETA = 10000.0
HA_Q = 8
HA_KV = 2
HD_A = 64
WINDOW = 128
N_BUCKETS = 32
MAX_DISTANCE = 128
HB = 8
Q_LORA = 256
KV_LORA = 128
NOPE_B = 64
ROPE_B = 32
V_B = 64
HC_Q = 8
HC_KV = 2
HD_C = 64
N_BRANCHES = 3
N_GROUPS = 4
EXP_PER_GROUP = 4
N_EXPERTS = N_GROUPS * EXP_PER_GROUP
D_EXPERT = 256
TOP_K_IN_GROUP = 2
IN_SPLITS = (HA_Q * HD_A, HA_KV * HD_A, HA_KV * HD_A,
             Q_LORA, KV_LORA + ROPE_B,
             HC_Q * HD_C, HC_KV * HD_C, HC_KV * HD_C,
             N_BRANCHES * D_MODEL)
IN_COLS = sum(IN_SPLITS)

kernel_name = "hybrid_gated_parallel_encoder"


def _split_points():
    pts, acc = [], 0
    for w in IN_SPLITS[:-1]:
        acc += w
        pts.append(acc)
    return pts


def rms_norm(x, g):
    x32 = x.astype(jnp.float32)
    y = x32 * lax.rsqrt(jnp.mean(x32 * x32, axis=-1, keepdims=True) + EPS)
    return (y * g.astype(jnp.float32)).astype(x.dtype)


def rope(x, pos):
    half = x.shape[-1] // 2
    inv = ROPE_THETA ** (-jnp.arange(half, dtype=jnp.float32) / half)
    ang = pos.astype(jnp.float32)[:, None] * inv[None, :]
    cos = jnp.cos(ang)[:, None, :]
    sin = jnp.sin(ang)[:, None, :]
    x32 = x.astype(jnp.float32)
    x1, x2 = x32[..., :half], x32[..., half:]
    return jnp.concatenate([x1 * cos - x2 * sin, x2 * cos + x1 * sin], axis=-1).astype(x.dtype)


def axial_rope(x, row, col):
    half = x.shape[-1] // 2
    return jnp.concatenate([rope(x[..., :half], row), rope(x[..., half:], col)], axis=-1)


def t5_bucket(rel):
    nb = N_BUCKETS // 2
    bucket = jnp.where(rel > 0, nb, 0)
    n = jnp.abs(rel)
    max_exact = nb // 2
    nf = jnp.maximum(n, 1).astype(jnp.float32)
    large = max_exact + (jnp.log(nf / max_exact) / math.log(MAX_DISTANCE / max_exact)
                         * (nb - max_exact)).astype(jnp.int32)
    large = jnp.minimum(large, nb - 1)
    return bucket + jnp.where(n < max_exact, n, large)


def window_attention(q, k, v, sink, rel_bias):
    bsz, s, hkv, grp, d = q.shape
    nb = s // BLOCK
    padw = ((0, 0), (BLOCK, BLOCK), (0, 0), (0, 0))
    kp = jnp.pad(k, padw)
    vp = jnp.pad(v, padw)
    qi = jnp.arange(BLOCK)[:, None]
    kj = jnp.arange(3 * BLOCK)[None, :]
    rel = kj - BLOCK - qi
    bias = rel_bias[t5_bucket(rel)].astype(jnp.float32)
    bias = jnp.transpose(bias.reshape(BLOCK, 3 * BLOCK, hkv, grp), (2, 3, 0, 1))
    band = jnp.abs(rel) <= WINDOW
    sink_col = jnp.broadcast_to(sink.astype(jnp.float32).reshape(1, hkv, grp, 1, 1),
                                (bsz, hkv, grp, BLOCK, 1))
    scale = HD_A ** -0.5

    def one_block(i):
        start = i * BLOCK
        qblk = lax.dynamic_slice_in_dim(q, start, BLOCK, axis=1)
        kblk = lax.dynamic_slice_in_dim(kp, start, 3 * BLOCK, axis=1)
        vblk = lax.dynamic_slice_in_dim(vp, start, 3 * BLOCK, axis=1)
        kpos = start - BLOCK + jnp.arange(3 * BLOCK)
        valid = band & ((kpos >= 0) & (kpos < s))[None, :]
        sc = jnp.einsum('bqhgd,bkhd->bhgqk', qblk, kblk).astype(jnp.float32) * scale + bias
        sc = jnp.where(valid, sc, NEG_INF)
        p = jax.nn.softmax(jnp.concatenate([sc, sink_col], axis=-1), axis=-1)[..., :-1]
        return jnp.einsum('bhgqk,bkhd->bqhgd', p.astype(vblk.dtype), vblk)

    o = lax.map(one_block, jnp.arange(nb))
    return jnp.moveaxis(o, 0, 1).reshape(bsz, s, hkv, grp, d)


def dense_block_attention(q, k, v, scale):
    bsz, s = q.shape[:2]
    nb = s // BLOCK
    qb = jnp.moveaxis(q.reshape(bsz, nb, BLOCK, *q.shape[2:]), 1, 0)

    def one_block(qblk):
        sc = jnp.einsum('bqhgd,bkhd->bhgqk', qblk, k).astype(jnp.float32) * scale
        p = jax.nn.softmax(sc, axis=-1)
        return jnp.einsum('bhgqk,bkhd->bqhgd', p.astype(v.dtype), v)

    o = lax.map(one_block, qb)
    return jnp.moveaxis(o, 0, 1).reshape(bsz, s, *o.shape[3:])


def mla_attention(c_q_raw, kv_a, q_norm_g, w_uq, kv_norm_g, w_ukv, pos):
    bsz, s, _ = c_q_raw.shape
    c_q = rms_norm(c_q_raw, q_norm_g)
    q = (c_q @ w_uq).reshape(bsz, s, HB, NOPE_B + ROPE_B)
    q_nope, q_pe = q[..., :NOPE_B], rope(q[..., NOPE_B:], pos)
    c_kv = rms_norm(kv_a[..., :KV_LORA], kv_norm_g)
    k_pe = rope(kv_a[..., KV_LORA:][:, :, None, :], pos)
    kv = (c_kv @ w_ukv).reshape(bsz, s, HB, NOPE_B + V_B)
    k_nope, v = kv[..., :NOPE_B], kv[..., NOPE_B:]
    qh = jnp.concatenate([q_nope, q_pe], axis=-1)[:, :, :, None, :]
    kh = jnp.concatenate([k_nope, jnp.broadcast_to(k_pe, (bsz, s, HB, ROPE_B))], axis=-1)
    o = dense_block_attention(qh, kh, v, (NOPE_B + ROPE_B) ** -0.5)
    return o.reshape(bsz, s, HB * V_B)


def axial_attention(c_q, c_k, c_v, q_norm_c, k_norm_c, row, col):
    bsz, s, _ = c_q.shape
    grp = HC_Q // HC_KV
    q = axial_rope(rms_norm(c_q.reshape(bsz, s, HC_Q, HD_C), q_norm_c), row, col)
    k = axial_rope(rms_norm(c_k.reshape(bsz, s, HC_KV, HD_C), k_norm_c), row, col)
    v = c_v.reshape(bsz, s, HC_KV, HD_C)
    o = dense_block_attention(q.reshape(bsz, s, HC_KV, grp, HD_C), k, v, HD_C ** -0.5)
    return o.reshape(bsz, s, HC_Q * HD_C)


def hier_moe(h, w_grp, b_grp, w_exr, b_exr, w_gate, w_up, w_down):
    bsz, s, d = h.shape
    t = h.reshape(bsz * s, d)
    g_logits = (t @ w_grp).astype(jnp.float32) + b_grp.astype(jnp.float32)
    g_prob = jax.nn.softmax(g_logits, axis=-1)
    _, grp = lax.top_k(g_logits, 1)
    grp_w = jnp.take_along_axis(g_prob, grp, axis=-1)
    e_logits = ((t @ w_exr).astype(jnp.float32) + b_exr.astype(jnp.float32)
                ).reshape(-1, N_GROUPS, EXP_PER_GROUP)
    e_sel = jnp.take_along_axis(e_logits, grp[:, :, None], axis=1)[:, 0]
    top_v, top_i = lax.top_k(e_sel, TOP_K_IN_GROUP)
    top_w = jax.nn.softmax(top_v, axis=-1) * grp_w
    expert_id = grp * EXP_PER_GROUP + top_i
    combine = jnp.sum(jax.nn.one_hot(expert_id, N_EXPERTS, dtype=jnp.float32) * top_w[..., None],
                      axis=1).astype(h.dtype)
    out = jnp.zeros_like(t)
    for e in range(N_EXPERTS):
        a = jax.nn.silu(t @ w_gate[e]) * (t @ w_up[e])
        out = out + combine[:, e:e + 1] * (a @ w_down[e])
    return out.reshape(bsz, s, d)


def encoder_layer(x, pos, row, col, rel_bias, ln1_g, w_in, sink_a, q_norm_g, w_uq, kv_norm_g, w_ukv,
                  q_norm_c, k_norm_c, w_br_a, w_br_b, w_br_c, w_out, ln2_g, w_grp, b_grp, w_exr,
                  b_exr, w_gate, w_up, w_down):
    bsz, s, _ = x.shape
    h = rms_norm(x, ln1_g)
    proj = h @ w_in
    a_q, a_k, a_v, b_dq, b_dkv, c_q, c_k, c_v, gate_logits = jnp.split(proj, _split_points(), axis=-1)
    ga = HA_Q // HA_KV
    ya = window_attention(a_q.reshape(bsz, s, HA_KV, ga, HD_A), a_k.reshape(bsz, s, HA_KV, HD_A),
                          a_v.reshape(bsz, s, HA_KV, HD_A), sink_a, rel_bias)
    ya = ya.reshape(bsz, s, HA_Q * HD_A) @ w_br_a
    yb = mla_attention(b_dq, b_dkv, q_norm_g, w_uq, kv_norm_g, w_ukv, pos) @ w_br_b
    yc = axial_attention(c_q, c_k, c_v, q_norm_c, k_norm_c, row, col) @ w_br_c
    gates = jax.nn.sigmoid(gate_logits).reshape(bsz, s, N_BRANCHES, D_MODEL)
    merged = gates[:, :, 0] * ya + gates[:, :, 1] * yb + gates[:, :, 2] * yc
    x = x + merged @ w_out
    x = x + hier_moe(rms_norm(x, ln2_g), w_grp, b_grp, w_exr, b_exr, w_gate, w_up, w_down)
    return x


def trunk(x, rel_bias, final_g, ln1_g, w_in, sink_a, q_norm_g, w_uq, kv_norm_g, w_ukv, q_norm_c,
          k_norm_c, w_br_a, w_br_b, w_br_c, w_out, ln2_g, w_grp, b_grp, w_exr, b_exr, w_gate, w_up,
          w_down):
    s = x.shape[1]
    rows = s // GRID_W
    pos = jnp.arange(s, dtype=jnp.int32)
    row = jnp.repeat(jnp.arange(rows, dtype=jnp.int32), GRID_W)
    col = jnp.tile(jnp.arange(GRID_W, dtype=jnp.int32), rows)
    for l in range(DEPTH):
        x = encoder_layer(x, pos, row, col, rel_bias, ln1_g[l], w_in[l], sink_a[l], q_norm_g[l],
                          w_uq[l], kv_norm_g[l], w_ukv[l], q_norm_c[l], k_norm_c[l], w_br_a[l],
                          w_br_b[l], w_br_c[l], w_out[l], ln2_g[l], w_grp[l], b_grp[l], w_exr[l],
                          b_exr[l], w_gate[l], w_up[l], w_down[l])
    return rms_norm(x, final_g)


def _w(key, shape, fan_in):
    return jax.random.normal(key, shape, jnp.float32) * (fan_in ** -0.5)


def _gain(key, shape):
    return 1.0 + 0.05 * jax.random.normal(key, shape, jnp.float32)


def setup_inputs(seed: int = 0) -> dict:
    key = jax.random.key(seed)
    ks = jax.random.split(key, 25)
    L = DEPTH
    nrm = jax.random.normal
    return {
        "x_prompt": nrm(ks[0], (BATCH, SEQ, D_MODEL), jnp.float32),
        "x_sample": nrm(ks[1], (DEC_BATCH, DEC_SEQ, D_MODEL), jnp.float32),
        "rel_bias": 0.5 * nrm(ks[2], (N_BUCKETS, HA_Q), jnp.float32),
        "final_g": _gain(ks[3], (D_MODEL,)),
        "ln1_g": _gain(ks[4], (L, D_MODEL)),
        "w_in": _w(ks[5], (L, D_MODEL, IN_COLS), D_MODEL),
        "sink_a": 0.5 * nrm(ks[6], (L, HA_Q), jnp.float32),
        "q_norm_g": _gain(ks[7], (L, Q_LORA)),
        "w_uq": _w(ks[8], (L, Q_LORA, HB * (NOPE_B + ROPE_B)), Q_LORA),
        "kv_norm_g": _gain(ks[9], (L, KV_LORA)),
        "w_ukv": _w(ks[10], (L, KV_LORA, HB * (NOPE_B + V_B)), KV_LORA),
        "q_norm_c": _gain(ks[11], (L, HD_C)),
        "k_norm_c": _gain(ks[12], (L, HD_C)),
        "w_br_a": _w(ks[13], (L, HA_Q * HD_A, D_MODEL), HA_Q * HD_A),
        "w_br_b": _w(ks[14], (L, HB * V_B, D_MODEL), HB * V_B),
        "w_br_c": _w(ks[15], (L, HC_Q * HD_C, D_MODEL), HC_Q * HD_C),
        "w_out": _w(ks[16], (L, D_MODEL, D_MODEL), D_MODEL),
        "ln2_g": _gain(ks[17], (L, D_MODEL)),
        "w_grp": _w(ks[18], (L, D_MODEL, N_GROUPS), D_MODEL),
        "b_grp": 0.01 * nrm(ks[19], (L, N_GROUPS), jnp.float32),
        "w_exr": _w(ks[20], (L, D_MODEL, N_EXPERTS), D_MODEL),
        "b_exr": 0.01 * nrm(ks[21], (L, N_EXPERTS), jnp.float32),
        "w_gate": _w(ks[22], (L, N_EXPERTS, D_MODEL, D_EXPERT), D_MODEL),
        "w_up": _w(ks[23], (L, N_EXPERTS, D_MODEL, D_EXPERT), D_MODEL),
        "w_down": _w(ks[24], (L, N_EXPERTS, D_EXPERT, D_MODEL), D_EXPERT),
    }


def reference(x_prompt, x_sample, rel_bias, final_g, ln1_g, w_in, sink_a, q_norm_g, w_uq, kv_norm_g,
              w_ukv, q_norm_c, k_norm_c, w_br_a, w_br_b, w_br_c, w_out, ln2_g, w_grp, b_grp, w_exr,
              b_exr, w_gate, w_up, w_down):
    y_prompt = trunk(x_prompt, rel_bias, final_g, ln1_g, w_in, sink_a, q_norm_g, w_uq, kv_norm_g,
                     w_ukv, q_norm_c, k_norm_c, w_br_a, w_br_b, w_br_c, w_out, ln2_g, w_grp, b_grp,
                     w_exr, b_exr, w_gate, w_up, w_down)
    y_sample = trunk(x_sample, rel_bias, final_g, ln1_g, w_in, sink_a, q_norm_g, w_uq, kv_norm_g,
                     w_ukv, q_norm_c, k_norm_c, w_br_a, w_br_b, w_br_c, w_out, ln2_g, w_grp, b_grp,
                     w_exr, b_exr, w_gate, w_up, w_down)
    return (y_prompt, y_sample)
```

```python
import functools
import math

import jax
import jax.numpy as jnp
from jax import lax
from jax.experimental import pallas as pl
from jax.experimental.pallas import tpu as pltpu

F32 = jnp.float32
BF16 = jnp.bfloat16

D_MODEL = 1024
GRID_W = 64
BLOCK = 128
EPS = 1e-6
NEG_INF = -1e30
ROPE_THETA = 10000.0
HA_Q, HA_KV, HD_A = 8, 2, 64
WINDOW = 128
N_BUCKETS = 32
MAX_DISTANCE = 128
HB, Q_LORA, KV_LORA, NOPE_B, ROPE_B, V_B = 8, 256, 128, 64, 32, 64
HC_Q, HC_KV, HD_C = 8, 2, 64
N_BRANCHES = 3
N_GROUPS, EXP_PER_GROUP = 4, 4
N_EXPERTS = N_GROUPS * EXP_PER_GROUP
D_EXPERT = 256

LANE = 128
HEAD_PAD = 128
ATT_COLS = 2176
ROUTER_COLS = 128
GROUP_LANE0 = 0
EXPERT_LANE0 = N_GROUPS

TM_PREP = 256
TK_DENSE = TM_PREP
TQ_DENSE = 512
TM_MERGE = 256
TM_MOE = 512
VMEM_LIMIT = 56 * 1024 * 1024


def _rms(x, g):
    return x * lax.rsqrt(jnp.mean(x * x, axis=-1, keepdims=True) + EPS) * g


def _swap16(x):
    n = x.shape[-1]
    lane = lax.broadcasted_iota(jnp.int32, x.shape, x.ndim - 1)
    return jnp.where((lane & 16) == 0, pltpu.roll(x, n - 16, x.ndim - 1), pltpu.roll(x, 16, x.ndim - 1))


def _tile_lanes(t, reps):
    return jnp.concatenate([t] * reps, axis=-1) if reps > 1 else t


def _group_ssq(x, g_ref):
    sq = x * x
    hi = sq.astype(BF16)
    lo = (sq - hi.astype(F32)).astype(BF16)
    g = g_ref[...]
    return (jnp.dot(hi, g, preferred_element_type=F32) + jnp.dot(lo, g, preferred_element_type=F32))


def _prep_kernel(x_ref, g1_ref, w_ref, qng_ref, wuq_ref, kvng_ref, wuk_ref, wuv_ref,
                 gqc_ref, gkc_ref, gmat_ref, bqc_ref, bqs_ref, bkc_ref, bks_ref,
                 cqc_ref, cqs_ref, ckc_ref, cks_ref,
                 qa_ref, ka_ref, va_ref, qbt_ref, kb_ref, vbt_ref, qct_ref, kc_ref, vct_ref):
    x = x_ref[0]
    hb = _rms(x, g1_ref[...]).astype(BF16)

    pa = jnp.dot(hb, w_ref[:, 0:768], preferred_element_type=F32)
    qa_ref[0] = (pa[:, 0:512] * (HD_A ** -0.5)).astype(BF16)
    ka_ref[0] = pa[:, 512:640].astype(BF16)
    va_ref[0] = pa[:, 640:768].astype(BF16)

    pb = jnp.dot(hb, w_ref[:, 768:1280], preferred_element_type=F32)
    cq = _rms(pb[:, 0:256], qng_ref[...]).astype(BF16)
    qb = jnp.dot(cq, wuq_ref[...], preferred_element_type=F32)
    qb = qb * _tile_lanes(bqc_ref[...], HB) + _swap16(qb) * _tile_lanes(bqs_ref[...], HB)
    qbt_ref[0] = qb.T.astype(BF16)
    ckv = _rms(pb[:, 256:384], kvng_ref[...]).astype(BF16)
    kpe = pb[:, 384:512]
    kpe = kpe * bkc_ref[...] + _swap16(kpe) * bks_ref[...]
    kb = jnp.dot(ckv, wuk_ref[...], preferred_element_type=F32) + _tile_lanes(kpe, HB)
    kb_ref[0] = kb.astype(BF16)
    vb = jnp.dot(ckv, wuv_ref[...], preferred_element_type=F32)
    vbt_ref[0, 0] = vb.T.astype(BF16)

    pc = jnp.dot(hb, w_ref[:, 1280:2176], preferred_element_type=F32)
    qc = pc[:, 0:512]
    qc = qc * lax.rsqrt(_group_ssq(qc, gmat_ref) * (1.0 / HD_C) + EPS) * gqc_ref[...]
    qc = qc * _tile_lanes(cqc_ref[...], 4) + _swap16(qc) * _tile_lanes(cqs_ref[...], 4)
    qct_ref[0] = qc.T.astype(BF16)
    kc = pc[:, 512:768]
    kc = kc * lax.rsqrt(_group_ssq(kc, gmat_ref.at[0:256, 0:256]) * (1.0 / HD_C) + EPS) * gkc_ref[...]
    kc = kc * _tile_lanes(ckc_ref[...], 2) + _swap16(kc) * _tile_lanes(cks_ref[...], 2)
    kc_ref[0] = kc.astype(BF16)
    vct_ref[0, 0] = pc[:, 768:896].T.astype(BF16)


def _prep_call(x, lw, tabs):
    b, s, d = x.shape
    tm = TM_PREP
    nt = s // tm
    const = lambda *shape: pl.BlockSpec(shape, lambda bi, ti: (0,) * len(shape))
    tab = lambda w: pl.BlockSpec((tm, w), lambda bi, ti: (ti, 0))
    in_specs = [
        pl.BlockSpec((1, tm, d), lambda bi, ti: (bi, ti, 0)),
        const(1, d), const(d, ATT_COLS), const(1, Q_LORA), const(Q_LORA, HB * HEAD_PAD),
        const(1, KV_LORA), const(KV_LORA, HB * HEAD_PAD), const(KV_LORA, HB * V_B),
        const(1, 512), const(1, 256), const(512, 512),
        tab(LANE), tab(LANE), tab(LANE), tab(LANE), tab(LANE), tab(LANE), tab(LANE), tab(LANE),
    ]
    tok = lambda w: pl.BlockSpec((1, tm, w), lambda bi, ti: (bi, ti, 0))
    tr = lambda w: pl.BlockSpec((1, w, tm), lambda bi, ti: (bi, 0, ti))
    chunked = lambda w: pl.BlockSpec((1, 1, w, tm), lambda bi, ti: (bi, ti, 0, 0))
    out_specs = [tok(512), tok(128), tok(128), tr(HB * HEAD_PAD), tok(HB * HEAD_PAD), chunked(HB * V_B),
                 tr(512), tok(256), chunked(128)]
    out_shape = [
        jax.ShapeDtypeStruct((b, s, 512), BF16), jax.ShapeDtypeStruct((b, s, 128), BF16),
        jax.ShapeDtypeStruct((b, s, 128), BF16),
        jax.ShapeDtypeStruct((b, HB * HEAD_PAD, s), BF16), jax.ShapeDtypeStruct((b, s, HB * HEAD_PAD), BF16),
        jax.ShapeDtypeStruct((b, nt, HB * V_B, tm), BF16),
        jax.ShapeDtypeStruct((b, 512, s), BF16), jax.ShapeDtypeStruct((b, s, 256), BF16),
        jax.ShapeDtypeStruct((b, nt, 128, tm), BF16),
    ]
    return pl.pallas_call(
        _prep_kernel, grid=(b, nt), in_specs=in_specs, out_specs=out_specs, out_shape=out_shape,
        compiler_params=pltpu.CompilerParams(dimension_semantics=("parallel", "parallel"),
                                             vmem_limit_bytes=VMEM_LIMIT),
        name="prep",
    )(x, lw["ln1_g"], lw["w_att"], lw["q_norm_g"], lw["w_uq"], lw["kv_norm_g"], lw["w_uk"], lw["w_uv"],
      lw["gqc"], lw["gkc"], tabs["gmat"], tabs["bqc"], tabs["bqs"], tabs["bkc"], tabs["bks"],
      tabs["cqc"], tabs["cqs"], tabs["ckc"], tabs["cks"])


def _window_kernel(sink_ref, q_ref, k_ref, v_ref, bias_ref, o_ref):
    s = k_ref.shape[1]
    i = pl.program_id(1)
    start = jnp.clip((i - 1) * BLOCK, 0, s - 3 * BLOCK)
    start = pl.multiple_of(start, BLOCK)
    kwin = k_ref[0, pl.ds(start, 3 * BLOCK), :]
    vwin = v_ref[0, pl.ds(start, 3 * BLOCK), :]
    grp = HA_Q // HA_KV
    outs = []
    for h in range(HA_Q):
        g = h // grp
        qh = q_ref[0, :, h * HD_A:(h + 1) * HD_A]
        kg = kwin[:, g * HD_A:(g + 1) * HD_A]
        vg = vwin[:, g * HD_A:(g + 1) * HD_A]
        sc = lax.dot_general(qh, kg, (((1,), (1,)), ((), ())), preferred_element_type=F32)
        sc = sc + bias_ref[0, h]
        sink = sink_ref[h]
        m = jnp.maximum(jnp.max(sc, axis=-1, keepdims=True), sink)
        p = jnp.exp(sc - m)
        l = jnp.sum(p, axis=-1, keepdims=True) + jnp.exp(sink - m)
        o = jnp.dot(p.astype(BF16), vg, preferred_element_type=F32)
        outs.append(o / l)
    o_ref[0] = jnp.concatenate(outs, axis=-1).astype(BF16)


def _window_call(qa, ka, va, sink, bias3):
    b, s, _ = qa.shape
    nb = s // BLOCK
    variant = lambda bi, i: (jnp.where(i == 0, 0, jnp.where(i == nb - 1, 2, 1)), 0, 0, 0)
    return pl.pallas_call(
        _window_kernel, grid=(b, nb),
        in_specs=[
            pl.BlockSpec(memory_space=pltpu.SMEM),
            pl.BlockSpec((1, BLOCK, 512), lambda bi, i: (bi, i, 0)),
            pl.BlockSpec((1, s, 128), lambda bi, i: (bi, 0, 0)),
            pl.BlockSpec((1, s, 128), lambda bi, i: (bi, 0, 0)),
            pl.BlockSpec((1, HA_Q, BLOCK, 3 * BLOCK), variant),
        ],
        out_specs=pl.BlockSpec((1, BLOCK, 512), lambda bi, i: (bi, i, 0)),
        out_shape=jax.ShapeDtypeStruct((b, s, 512), BF16),
        compiler_params=pltpu.CompilerParams(dimension_semantics=("parallel", "arbitrary"),
                                             vmem_limit_bytes=VMEM_LIMIT),
        name="window_attn",
    )(sink, qa, ka, va, bias3)


def _dense_kernel(qt_ref, k_ref, vt_ref, o_ref, ot_ref, *, n_heads, heads_per_kv, dq, dk, k_stride):
    nchunk = vt_ref.shape[1]
    tk = vt_ref.shape[3]
    tq = qt_ref.shape[2]
    for h in range(n_heads):
        g = h // heads_per_kv
        qt = qt_ref[0, h * dq:(h + 1) * dq, :][0:dk, :]

        def body(c, carry, g=g, qt=qt):
            m, l, acc = carry
            off = pl.multiple_of(c * tk, tk)
            kc = k_ref[0, pl.ds(off, tk), g * k_stride:g * k_stride + dk]
            st = jnp.dot(kc, qt, preferred_element_type=F32)
            m_new = jnp.maximum(m, jnp.max(st, axis=0, keepdims=True))
            alpha = jnp.exp(m - m_new)
            p = jnp.exp(st - m_new)
            l = alpha * l + jnp.sum(p, axis=0, keepdims=True)
            vc = vt_ref[0, c, g * 64:(g + 1) * 64, :]
            acc = alpha * acc + jnp.dot(vc, p.astype(BF16), preferred_element_type=F32)
            return m_new, l, acc

        init = (jnp.full((1, tq), -jnp.inf, F32), jnp.zeros((1, tq), F32), jnp.zeros((64, tq), F32))
        m, l, acc = lax.fori_loop(0, nchunk, body, init)
        ot_ref[h * 64:(h + 1) * 64, :] = acc / l
    o_ref[0] = ot_ref[...].T.astype(BF16)


def _dense_call(qt, k, vt, *, n_heads, heads_per_kv, dq, dk, k_stride, name):
    b, fq, s = qt.shape
    tq = TQ_DENSE
    fk = k.shape[2]
    _, nchunk, fv, tk = vt.shape
    kern = functools.partial(_dense_kernel, n_heads=n_heads, heads_per_kv=heads_per_kv, dq=dq, dk=dk,
                             k_stride=k_stride)
    return pl.pallas_call(
        kern, grid=(b, s // tq),
        in_specs=[
            pl.BlockSpec((1, fq, tq), lambda bi, i: (bi, 0, i)),
            pl.BlockSpec((1, s, fk), lambda bi, i: (bi, 0, 0)),
            pl.BlockSpec((1, nchunk, fv, tk), lambda bi, i: (bi, 0, 0, 0)),
        ],
        out_specs=pl.BlockSpec((1, tq, n_heads * 64), lambda bi, i: (bi, i, 0)),
        out_shape=jax.ShapeDtypeStruct((b, s, n_heads * 64), BF16),
        scratch_shapes=[pltpu.VMEM((n_heads * 64, tq), F32)],
        compiler_params=pltpu.CompilerParams(dimension_semantics=("parallel", "arbitrary"),
                                             vmem_limit_bytes=VMEM_LIMIT),
        name=name,
    )(qt, k, vt)


def _merge_kernel(x_ref, oa_ref, ob_ref, oc_ref, g1_ref, wg_ref, wa_ref, wb_ref, wc_ref, wo_ref,
                  g2_ref, wr_ref, br_ref, x1_ref, h2_ref, lg_ref):
    x = x_ref[...]
    hb = _rms(x, g1_ref[...]).astype(BF16)
    merged = None
    for j, (o_ref, w_ref) in enumerate(((oa_ref, wa_ref), (ob_ref, wb_ref), (oc_ref, wc_ref))):
        gl = jnp.dot(hb, wg_ref[:, j * D_MODEL:(j + 1) * D_MODEL], preferred_element_type=F32)
        y = jnp.dot(o_ref[...], w_ref[...], preferred_element_type=F32)
        t = jax.nn.sigmoid(gl) * y
        merged = t if merged is None else merged + t
    x1 = x + jnp.dot(merged.astype(BF16), wo_ref[...], preferred_element_type=F32)
    x1_ref[...] = x1
    h2 = _rms(x1, g2_ref[...])
    h2b = h2.astype(BF16)
    h2_ref[...] = h2b
    lg_ref[...] = jnp.dot(h2b, wr_ref[...], preferred_element_type=F32) + br_ref[...]


def _merge_call(x2d, oa, ob, oc, lw):
    t, d = x2d.shape
    tm = TM_MERGE
    const = lambda *shape: pl.BlockSpec(shape, lambda i: (0,) * len(shape))
    tok = lambda w: pl.BlockSpec((tm, w), lambda i: (i, 0))
    return pl.pallas_call(
        _merge_kernel, grid=(t // tm,),
        in_specs=[tok(d), tok(512), tok(512), tok(512), const(1, d), const(d, N_BRANCHES * d),
                  const(512, d), const(512, d), const(512, d), const(d, d), const(1, d),
                  const(d, ROUTER_COLS), const(1, ROUTER_COLS)],
        out_specs=[tok(d), tok(d), tok(ROUTER_COLS)],
        out_shape=[jax.ShapeDtypeStruct((t, d), F32), jax.ShapeDtypeStruct((t, d), BF16),
                   jax.ShapeDtypeStruct((t, ROUTER_COLS), F32)],
        compiler_params=pltpu.CompilerParams(dimension_semantics=("parallel",),
                                             vmem_limit_bytes=VMEM_LIMIT),
        name="merge",
    )(x2d, oa, ob, oc, lw["ln1_g"], lw["w_gate3"], lw["w_br_a"], lw["w_br_b"], lw["w_br_c"], lw["w_out"],
      lw["ln2_g"], lw["w_router"], lw["b_router"])


def _combine_weights(lg):
    lane = lax.broadcasted_iota(jnp.int32, lg.shape, 1)
    big = jnp.int32(1 << 20)
    ninf = jnp.float32(-jnp.inf)
    is_g = lane < N_GROUPS
    gl = jnp.where(is_g, lg, ninf)
    gmax = jnp.max(gl, axis=-1, keepdims=True)
    grp = jnp.min(jnp.where(is_g & (lg == gmax), lane, big), axis=-1, keepdims=True)
    gsum = jnp.sum(jnp.where(is_g, jnp.exp(gl - gmax), 0.0), axis=-1, keepdims=True)
    grp_w = 1.0 / gsum
    lo = EXPERT_LANE0 + grp * EXP_PER_GROUP
    in_grp = (lane >= lo) & (lane < lo + EXP_PER_GROUP)
    v1 = jnp.max(jnp.where(in_grp, lg, ninf), axis=-1, keepdims=True)
    i1 = jnp.min(jnp.where(in_grp & (lg == v1), lane, big), axis=-1, keepdims=True)
    rest = in_grp & (lane != i1)
    v2 = jnp.max(jnp.where(rest, lg, ninf), axis=-1, keepdims=True)
    i2 = jnp.min(jnp.where(rest & (lg == v2), lane, big), axis=-1, keepdims=True)
    e21 = jnp.exp(v2 - v1)
    den = 1.0 + e21
    w1 = (1.0 / den) * grp_w
    w2 = (e21 / den) * grp_w
    return jnp.where(lane == i1, w1, 0.0) + jnp.where(lane == i2, w2, 0.0)


def _moe_kernel(x1_ref, h2_ref, lg_ref, wgu_ref, wd_ref, gf_ref, o_ref, a_ref, *, final_norm):
    t = h2_ref[...]
    comb = _combine_weights(lg_ref[...])
    for e in range(N_EXPERTS):
        gu = jnp.dot(t, wgu_ref[e], preferred_element_type=F32)
        gt = gu[:, 0:D_EXPERT]
        a = gt * jax.nn.sigmoid(gt) * gu[:, D_EXPERT:2 * D_EXPERT]
        ce = comb[:, EXPERT_LANE0 + e:EXPERT_LANE0 + e + 1]
        a_ref[:, e * D_EXPERT:(e + 1) * D_EXPERT] = (a * ce).astype(BF16)
    y = x1_ref[...] + jnp.dot(a_ref[...], wd_ref[...], preferred_element_type=F32)
    if final_norm:
        y = _rms(y, gf_ref[...])
    o_ref[...] = y


def _moe_call(x1, h2, lg, lw, final_g, final_norm):
    t, d = x1.shape
    tm = TM_MOE
    tok = lambda w: pl.BlockSpec((tm, w), lambda i: (i, 0))
    once = pl.Buffered(1)
    return pl.pallas_call(
        functools.partial(_moe_kernel, final_norm=final_norm), grid=(t // tm,),
        in_specs=[tok(d), tok(d), tok(ROUTER_COLS),
                  pl.BlockSpec((N_EXPERTS, d, 2 * D_EXPERT), lambda i: (0, 0, 0), pipeline_mode=once),
                  pl.BlockSpec((N_EXPERTS * D_EXPERT, d), lambda i: (0, 0), pipeline_mode=once),
                  pl.BlockSpec((1, d), lambda i: (0, 0))],
        out_specs=tok(d),
        out_shape=jax.ShapeDtypeStruct((t, d), F32),
        scratch_shapes=[pltpu.VMEM((tm, N_EXPERTS * D_EXPERT), BF16)],
        compiler_params=pltpu.CompilerParams(dimension_semantics=("parallel",),
                                             vmem_limit_bytes=VMEM_LIMIT),
        name="moe",
    )(x1, h2, lg, lw["w_gu"], lw["w_dn"], final_g)


def _layer_weights(l, w_in, ln1_g, sink_a, q_norm_g, w_uq, kv_norm_g, w_ukv, q_norm_c, k_norm_c,
                   w_br_a, w_br_b, w_br_c, w_out, ln2_g, w_grp, b_grp, w_exr, b_exr, w_gate, w_up, w_down):
    wi = w_in[l]
    d = wi.shape[0]
    z = lambda n: jnp.zeros((d, n), F32)
    w_att = jnp.concatenate([
        wi[:, 0:1152], z(64), wi[:, 1152:1184], z(32), wi[:, 1184:1696],
        wi[:, 1696:1760], z(64), wi[:, 1760:1824], z(64), wi[:, 1824:1952]], axis=1).astype(BF16)
    wq = w_uq[l].reshape(Q_LORA, HB, NOPE_B + ROPE_B)
    wq = jnp.concatenate([wq, jnp.zeros((Q_LORA, HB, HEAD_PAD - NOPE_B - ROPE_B), F32)], axis=-1)
    wkv = w_ukv[l].reshape(KV_LORA, HB, NOPE_B + V_B)
    wk = jnp.concatenate([wkv[..., :NOPE_B], jnp.zeros((KV_LORA, HB, HEAD_PAD - NOPE_B), F32)], axis=-1)
    zero64 = jnp.zeros((HD_C,), F32)
    w_router = jnp.concatenate([w_grp[l], w_exr[l], jnp.zeros((d, ROUTER_COLS - N_GROUPS - N_EXPERTS), F32)], 1)
    b_router = jnp.concatenate([b_grp[l], b_exr[l], jnp.zeros((ROUTER_COLS - N_GROUPS - N_EXPERTS,), F32)])
    return dict(
        ln1_g=ln1_g[l].reshape(1, d), w_att=w_att, w_gate3=wi[:, 1952:].astype(BF16),
        sink=sink_a[l], q_norm_g=q_norm_g[l].reshape(1, Q_LORA),
        w_uq=wq.reshape(Q_LORA, HB * HEAD_PAD).astype(BF16),
        kv_norm_g=kv_norm_g[l].reshape(1, KV_LORA),
        w_uk=wk.reshape(KV_LORA, HB * HEAD_PAD).astype(BF16),
        w_uv=wkv[..., NOPE_B:].reshape(KV_LORA, HB * V_B).astype(BF16),
        gqc=jnp.tile(q_norm_c[l], HC_Q).reshape(1, 512),
        gkc=jnp.concatenate([k_norm_c[l], zero64, k_norm_c[l], zero64]).reshape(1, 256),
        w_br_a=w_br_a[l].astype(BF16), w_br_b=w_br_b[l].astype(BF16), w_br_c=w_br_c[l].astype(BF16),
        w_out=w_out[l].astype(BF16), ln2_g=ln2_g[l].reshape(1, d),
        w_router=w_router.astype(BF16), b_router=b_router.reshape(1, ROUTER_COLS),
        w_gu=jnp.concatenate([w_gate[l], w_up[l]], axis=-1).astype(BF16),
        w_dn=w_down[l].reshape(N_EXPERTS * D_EXPERT, d).astype(BF16),
    )


def _t5_bucket(rel):
    nb = N_BUCKETS // 2
    bucket = jnp.where(rel > 0, nb, 0)
    n = jnp.abs(rel)
    max_exact = nb // 2
    nf = jnp.maximum(n, 1).astype(F32)
    large = max_exact + (jnp.log(nf / max_exact) / math.log(MAX_DISTANCE / max_exact)
                         * (nb - max_exact)).astype(jnp.int32)
    large = jnp.minimum(large, nb - 1)
    return bucket + jnp.where(n < max_exact, n, large)


def _tables(s, rel_bias):
    half = ROPE_B // 2
    inv = ROPE_THETA ** (-jnp.arange(half, dtype=F32) / half)
    pos = jnp.arange(s, dtype=jnp.int32)
    row = (pos // GRID_W).astype(F32)
    col = (pos % GRID_W).astype(F32)

    def cs(p):
        ang = p[:, None] * inv[None, :]
        return jnp.cos(ang), jnp.sin(ang)

    cp, sp = cs(pos.astype(F32))
    cr, sr = cs(row)
    cc, sc = cs(col)
    one = jnp.ones((s, 64), F32)
    z32 = jnp.zeros((s, 32), F32)
    z64 = jnp.zeros((s, 64), F32)
    b_cos = jnp.concatenate([one, cp, cp, z32], axis=1)
    b_sin = jnp.concatenate([z64, -sp, sp, z32], axis=1)
    c_cos = jnp.concatenate([cr, cr, cc, cc], axis=1)
    c_sin = jnp.concatenate([-sr, sr, -sc, sc], axis=1)
    scale_b = (NOPE_B + ROPE_B) ** -0.5
    scale_c = HD_C ** -0.5
    gidx = jnp.arange(512) // HD_C
    gmat = (gidx[:, None] == gidx[None, :]).astype(BF16)
    qi = jnp.arange(BLOCK)[:, None]
    kj = jnp.arange(3 * BLOCK)[None, :]
    mats = []
    for shift in (0, BLOCK, 2 * BLOCK):
        rel = kj - shift - qi
        bias = rel_bias[_t5_bucket(rel)].astype(F32)
        bias = jnp.where((jnp.abs(rel) <= WINDOW)[:, :, None], bias, NEG_INF)
        mats.append(jnp.transpose(bias, (2, 0, 1)))
    return dict(
        bqc=b_cos * scale_b, bqs=b_sin * scale_b, bkc=b_cos, bks=b_sin,
        cqc=jnp.concatenate([c_cos, c_cos], 1) * scale_c, cqs=jnp.concatenate([c_sin, c_sin], 1) * scale_c,
        ckc=jnp.concatenate([c_cos, z64], 1), cks=jnp.concatenate([c_sin, z64], 1),
        gmat=gmat, bias3=jnp.stack(mats, axis=0))


def _trunk(x, layers, tabs, final_g):
    b, s, d = x.shape
    n_layers = len(layers)
    for l, lw in enumerate(layers):
        qa, ka, va, qbt, kb, vbt, qct, kc, vct = _prep_call(x, lw, tabs)
        oa = _window_call(qa, ka, va, lw["sink"], tabs["bias3"])
        ob = _dense_call(qbt, kb, vbt, n_heads=HB, heads_per_kv=1, dq=HEAD_PAD, dk=HEAD_PAD,
                         k_stride=HEAD_PAD, name="latent_attn")
        oc = _dense_call(qct, kc, vct, n_heads=HC_Q, heads_per_kv=HC_Q // HC_KV, dq=HD_C, dk=HD_C,
                         k_stride=LANE, name="axial_attn")
        t = b * s
        x1, h2, lg = _merge_call(x.reshape(t, d), oa.reshape(t, 512), ob.reshape(t, 512),
                                 oc.reshape(t, 512), lw)
        x = _moe_call(x1, h2, lg, lw, final_g, l == n_layers - 1).reshape(b, s, d)
    return x


def kernel(x_prompt, x_sample, rel_bias, final_g, ln1_g, w_in, sink_a, q_norm_g, w_uq, kv_norm_g, w_ukv,
           q_norm_c, k_norm_c, w_br_a, w_br_b, w_br_c, w_out, ln2_g, w_grp, b_grp, w_exr, b_exr, w_gate,
           w_up, w_down):
    n_layers = w_in.shape[0]
    layers = [_layer_weights(l, w_in, ln1_g, sink_a, q_norm_g, w_uq, kv_norm_g, w_ukv, q_norm_c, k_norm_c,
                             w_br_a, w_br_b, w_br_c, w_out, ln2_g, w_grp, b_grp, w_exr, b_exr, w_gate,
                             w_up, w_down) for l in range(n_layers)]
    fg = final_g.reshape(1, -1)
    outs = []
    for x in (x_prompt, x_sample):
        tabs = _tables(x.shape[1], rel_bias)
        outs.append(_trunk(x, layers, tabs, fg))
    return tuple(outs)
```

```python
import functools
import math

import jax
import jax.numpy as jnp
from jax import lax
from jax.experimental import pallas as pl
from jax.experimental.pallas import tpu as pltpu

F32 = jnp.float32
BF16 = jnp.bfloat16

D_MODEL = 1024
GRID_W = 64
BLOCK = 128
EPS = 1e-6
NEG_INF = -1e30
ROPE_THETA = 10000.0
HA_Q, HA_KV, HD_A = 8, 2, 64
WINDOW = 128
N_BUCKETS = 32
MAX_DISTANCE = 128
HB, Q_LORA, KV_LORA, NOPE_B, ROPE_B, V_B = 8, 256, 128, 64, 32, 64
HC_Q, HC_KV, HD_C = 8, 2, 64
N_BRANCHES = 3
N_GROUPS, EXP_PER_GROUP = 4, 4
N_EXPERTS = N_GROUPS * EXP_PER_GROUP
D_EXPERT = 256

LANE = 128
HEAD_PAD = 128
ATT_COLS = 2176
ROUTER_COLS = 128
GROUP_LANE0 = 0
EXPERT_LANE0 = N_GROUPS

TM_PREP = 256
TK_DENSE = TM_PREP
TQ_DENSE = 512
CHUNKS_PER_TRIP = 2
SCORE_LOOKAHEAD = 2
ACC_ROWS = 80
LOG2E = math.log2(math.e)
TM_MERGE = 256
TM_MOE = 512
VMEM_LIMIT = 56 * 1024 * 1024


def _rms(x, g):
    return x * lax.rsqrt(jnp.mean(x * x, axis=-1, keepdims=True) + EPS) * g


def _swap16(x):
    n = x.shape[-1]
    lane = lax.broadcasted_iota(jnp.int32, x.shape, x.ndim - 1)
    return jnp.where((lane & 16) == 0, pltpu.roll(x, n - 16, x.ndim - 1), pltpu.roll(x, 16, x.ndim - 1))


def _tile_lanes(t, reps):
    return jnp.concatenate([t] * reps, axis=-1) if reps > 1 else t


def _group_ssq(x, g_ref):
    sq = x * x
    hi = sq.astype(BF16)
    lo = (sq - hi.astype(F32)).astype(BF16)
    g = g_ref[...]
    return (jnp.dot(hi, g, preferred_element_type=F32) + jnp.dot(lo, g, preferred_element_type=F32))


def _prep_kernel(x_ref, g1_ref, w_ref, qng_ref, wuq_ref, kvng_ref, wuk_ref, wuv_ref,
                 gqc_ref, gkc_ref, gmat_ref, bqc_ref, bqs_ref, bkc_ref, bks_ref,
                 cqc_ref, cqs_ref, ckc_ref, cks_ref,
                 qa_ref, ka_ref, va_ref, qbt_ref, kb_ref, vbt_ref, qct_ref, kc_ref, vct_ref):
    x = x_ref[0]
    hb = _rms(x, g1_ref[...]).astype(BF16)

    pa = jnp.dot(hb, w_ref[:, 0:768], preferred_element_type=F32)
    qa_ref[0] = (pa[:, 0:512] * (HD_A ** -0.5)).astype(BF16)
    ka_ref[0] = pa[:, 512:640].astype(BF16)
    va_ref[0] = pa[:, 640:768].astype(BF16)

    pb = jnp.dot(hb, w_ref[:, 768:1280], preferred_element_type=F32)
    cq = _rms(pb[:, 0:256], qng_ref[...]).astype(BF16)
    qb = jnp.dot(cq, wuq_ref[...], preferred_element_type=F32)
    qb = qb * _tile_lanes(bqc_ref[...], HB) + _swap16(qb) * _tile_lanes(bqs_ref[...], HB)
    qbt_ref[0] = qb.T.astype(BF16)
    ckv = _rms(pb[:, 256:384], kvng_ref[...]).astype(BF16)
    kpe = pb[:, 384:512]
    kpe = kpe * bkc_ref[...] + _swap16(kpe) * bks_ref[...]
    kb = jnp.dot(ckv, wuk_ref[...], preferred_element_type=F32) + _tile_lanes(kpe, HB)
    kb_ref[0] = kb.astype(BF16)
    vb = jnp.dot(ckv, wuv_ref[...], preferred_element_type=F32)
    vbt_ref[0, 0] = vb.T.astype(BF16)

    pc = jnp.dot(hb, w_ref[:, 1280:2176], preferred_element_type=F32)
    qc = pc[:, 0:512]
    qc = qc * lax.rsqrt(_group_ssq(qc, gmat_ref) * (1.0 / HD_C) + EPS) * gqc_ref[...]
    qc = qc * _tile_lanes(cqc_ref[...], 4) + _swap16(qc) * _tile_lanes(cqs_ref[...], 4)
    qct_ref[0] = qc.T.astype(BF16)
    kc = pc[:, 512:768]
    kc = kc * lax.rsqrt(_group_ssq(kc, gmat_ref.at[0:256, 0:256]) * (1.0 / HD_C) + EPS) * gkc_ref[...]
    kc = kc * _tile_lanes(ckc_ref[...], 2) + _swap16(kc) * _tile_lanes(cks_ref[...], 2)
    kc_ref[0] = kc.astype(BF16)
    vct_ref[0, 0] = pc[:, 768:896].T.astype(BF16)


def _prep_call(x, lw, tabs):
    b, s, d = x.shape
    tm = TM_PREP
    nt = s // tm
    const = lambda *shape: pl.BlockSpec(shape, lambda bi, ti: (0,) * len(shape))
    tab = lambda w: pl.BlockSpec((tm, w), lambda bi, ti: (ti, 0))
    in_specs = [
        pl.BlockSpec((1, tm, d), lambda bi, ti: (bi, ti, 0)),
        const(1, d), const(d, ATT_COLS), const(1, Q_LORA), const(Q_LORA, HB * HEAD_PAD),
        const(1, KV_LORA), const(KV_LORA, HB * HEAD_PAD), const(KV_LORA, HB * V_B),
        const(1, 512), const(1, 256), const(512, 512),
        tab(LANE), tab(LANE), tab(LANE), tab(LANE), tab(LANE), tab(LANE), tab(LANE), tab(LANE),
    ]
    tok = lambda w: pl.BlockSpec((1, tm, w), lambda bi, ti: (bi, ti, 0))
    tr = lambda w: pl.BlockSpec((1, w, tm), lambda bi, ti: (bi, 0, ti))
    chunked = lambda w: pl.BlockSpec((1, 1, w, tm), lambda bi, ti: (bi, ti, 0, 0))
    out_specs = [tok(512), tok(128), tok(128), tr(HB * HEAD_PAD), tok(HB * HEAD_PAD), chunked(HB * V_B),
                 tr(512), tok(256), chunked(128)]
    out_shape = [
        jax.ShapeDtypeStruct((b, s, 512), BF16), jax.ShapeDtypeStruct((b, s, 128), BF16),
        jax.ShapeDtypeStruct((b, s, 128), BF16),
        jax.ShapeDtypeStruct((b, HB * HEAD_PAD, s), BF16), jax.ShapeDtypeStruct((b, s, HB * HEAD_PAD), BF16),
        jax.ShapeDtypeStruct((b, nt, HB * V_B, tm), BF16),
        jax.ShapeDtypeStruct((b, 512, s), BF16), jax.ShapeDtypeStruct((b, s, 256), BF16),
        jax.ShapeDtypeStruct((b, nt, 128, tm), BF16),
    ]
    return pl.pallas_call(
        _prep_kernel, grid=(b, nt), in_specs=in_specs, out_specs=out_specs, out_shape=out_shape,
        compiler_params=pltpu.CompilerParams(dimension_semantics=("parallel", "parallel"),
                                             vmem_limit_bytes=VMEM_LIMIT),
        name="prep",
    )(x, lw["ln1_g"], lw["w_att"], lw["q_norm_g"], lw["w_uq"], lw["kv_norm_g"], lw["w_uk"], lw["w_uv"],
      lw["gqc"], lw["gkc"], tabs["gmat"], tabs["bqc"], tabs["bqs"], tabs["bkc"], tabs["bks"],
      tabs["cqc"], tabs["cqs"], tabs["ckc"], tabs["cks"])


def _window_kernel(sink_ref, q_ref, k_ref, v_ref, bias_ref, o_ref):
    s = k_ref.shape[1]
    i = pl.program_id(1)
    start = jnp.clip((i - 1) * BLOCK, 0, s - 3 * BLOCK)
    start = pl.multiple_of(start, BLOCK)
    kwin = k_ref[0, pl.ds(start, 3 * BLOCK), :]
    vwin = v_ref[0, pl.ds(start, 3 * BLOCK), :]
    grp = HA_Q // HA_KV
    outs = []
    for h in range(HA_Q):
        g = h // grp
        qh = q_ref[0, :, h * HD_A:(h + 1) * HD_A]
        kg = kwin[:, g * HD_A:(g + 1) * HD_A]
        vg = vwin[:, g * HD_A:(g + 1) * HD_A]
        sc = lax.dot_general(qh, kg, (((1,), (1,)), ((), ())), preferred_element_type=F32)
        sc = sc + bias_ref[0, h]
        sink = sink_ref[h]
        m = jnp.maximum(jnp.max(sc, axis=-1, keepdims=True), sink)
        p = jnp.exp(sc - m)
        l = jnp.sum(p, axis=-1, keepdims=True) + jnp.exp(sink - m)
        o = jnp.dot(p.astype(BF16), vg, preferred_element_type=F32)
        outs.append(o / l)
    o_ref[0] = jnp.concatenate(outs, axis=-1).astype(BF16)


def _window_call(qa, ka, va, sink, bias3):
    b, s, _ = qa.shape
    nb = s // BLOCK
    variant = lambda bi, i: (jnp.where(i == 0, 0, jnp.where(i == nb - 1, 2, 1)), 0, 0, 0)
    return pl.pallas_call(
        _window_kernel, grid=(b, nb),
        in_specs=[
            pl.BlockSpec(memory_space=pltpu.SMEM),
            pl.BlockSpec((1, BLOCK, 512), lambda bi, i: (bi, i, 0)),
            pl.BlockSpec((1, s, 128), lambda bi, i: (bi, 0, 0)),
            pl.BlockSpec((1, s, 128), lambda bi, i: (bi, 0, 0)),
            pl.BlockSpec((1, HA_Q, BLOCK, 3 * BLOCK), variant),
        ],
        out_specs=pl.BlockSpec((1, BLOCK, 512), lambda bi, i: (bi, i, 0)),
        out_shape=jax.ShapeDtypeStruct((b, s, 512), BF16),
        compiler_params=pltpu.CompilerParams(dimension_semantics=("parallel", "arbitrary"),
                                             vmem_limit_bytes=VMEM_LIMIT),
        name="window_attn",
    )(sink, qa, ka, va, bias3)


def _dense_kernel(qt_ref, k_ref, vt_ref, o_ref, acc_ref, ot_ref, st_ref, *, n_heads, heads_per_kv, dq, dk,
                  k_stride):
    nchunk = vt_ref.shape[1]
    tkc = vt_ref.shape[3]
    tq = qt_ref.shape[2]
    tk = CHUNKS_PER_TRIP * tkc
    ntrip = nchunk // CHUNKS_PER_TRIP
    ones_rows = (lax.broadcasted_iota(jnp.int32, (ACC_ROWS - 64, tkc), 0) == 0).astype(BF16)

    def scores(t, h):
        g = h // heads_per_kv
        off = pl.multiple_of(t * tk, tk)
        qt = qt_ref[0, h * dq:h * dq + dk, :]
        kc = k_ref[0, pl.ds(off, tk), g * k_stride:g * k_stride + dk]
        return jnp.dot(kc, qt, preferred_element_type=F32)

    acc_ref[...] = jnp.zeros(acc_ref.shape, F32)
    for j in range(SCORE_LOOKAHEAD):
        st_ref[j] = scores(0, j)

    def body(t, ms):
        t_next = jnp.minimum(t + 1, ntrip - 1)
        pending = [st_ref[j] for j in range(SCORE_LOOKAHEAD)]
        new_ms = []
        for h in range(n_heads):
            g = h // heads_per_kv
            st = pending.pop(0)
            ahead = h + SCORE_LOOKAHEAD
            if ahead < n_heads:
                pending.append(scores(t, ahead))
            else:
                st_ref[ahead - n_heads] = scores(t_next, ahead - n_heads)
            m_new = jnp.maximum(ms[h], jnp.max(st, axis=0, keepdims=True))
            alpha = jnp.exp2(ms[h] - m_new)
            p = jnp.exp2(st - m_new).astype(BF16)
            pv = None
            for cc in range(CHUNKS_PER_TRIP):
                vc = vt_ref[0, t * CHUNKS_PER_TRIP + cc, g * 64:(g + 1) * 64, :]
                part = jnp.dot(jnp.concatenate([vc, ones_rows], axis=0), p[cc * tkc:(cc + 1) * tkc, :],
                               preferred_element_type=F32)
                pv = part if pv is None else pv + part
            rows = slice(h * ACC_ROWS, (h + 1) * ACC_ROWS)
            acc_ref[rows, :] = alpha * acc_ref[rows, :] + pv
            new_ms.append(m_new)
        return tuple(new_ms)

    lax.fori_loop(0, ntrip, body, (jnp.full((1, tq), -jnp.inf, F32),) * n_heads)
    for h in range(n_heads):
        num = acc_ref[h * ACC_ROWS:h * ACC_ROWS + 64, :]
        den = acc_ref[h * ACC_ROWS + 64:h * ACC_ROWS + 65, :]
        ot_ref[h * 64:(h + 1) * 64, :] = num / den
    o_ref[0] = ot_ref[...].T.astype(BF16)


def _dense_call(qt, k, vt, *, n_heads, heads_per_kv, dq, dk, k_stride, name):
    b, fq, s = qt.shape
    tq = TQ_DENSE
    fk = k.shape[2]
    _, nchunk, fv, tk = vt.shape
    kern = functools.partial(_dense_kernel, n_heads=n_heads, heads_per_kv=heads_per_kv, dq=dq, dk=dk,
                             k_stride=k_stride)
    return pl.pallas_call(
        kern, grid=(b, s // tq),
        in_specs=[
            pl.BlockSpec((1, fq, tq), lambda bi, i: (bi, 0, i)),
            pl.BlockSpec((1, s, fk), lambda bi, i: (bi, 0, 0)),
            pl.BlockSpec((1, nchunk, fv, tk), lambda bi, i: (bi, 0, 0, 0)),
        ],
        out_specs=pl.BlockSpec((1, tq, n_heads * 64), lambda bi, i: (bi, i, 0)),
        out_shape=jax.ShapeDtypeStruct((b, s, n_heads * 64), BF16),
        scratch_shapes=[pltpu.VMEM((n_heads * ACC_ROWS, tq), F32), pltpu.VMEM((n_heads * 64, tq), F32),
                        pltpu.VMEM((SCORE_LOOKAHEAD, CHUNKS_PER_TRIP * tk, tq), F32)],
        compiler_params=pltpu.CompilerParams(dimension_semantics=("parallel", "arbitrary"),
                                             vmem_limit_bytes=VMEM_LIMIT),
        name=name,
    )(qt, k, vt)


def _merge_kernel(x_ref, oa_ref, ob_ref, oc_ref, g1_ref, wg_ref, wa_ref, wb_ref, wc_ref, wo_ref,
                  g2_ref, wr_ref, br_ref, x1_ref, h2_ref, lg_ref):
    x = x_ref[...]
    hb = _rms(x, g1_ref[...]).astype(BF16)
    merged = None
    for j, (o_ref, w_ref) in enumerate(((oa_ref, wa_ref), (ob_ref, wb_ref), (oc_ref, wc_ref))):
        gl = jnp.dot(hb, wg_ref[:, j * D_MODEL:(j + 1) * D_MODEL], preferred_element_type=F32)
        y = jnp.dot(o_ref[...], w_ref[...], preferred_element_type=F32)
        t = jax.nn.sigmoid(gl) * y
        merged = t if merged is None else merged + t
    x1 = x + jnp.dot(merged.astype(BF16), wo_ref[...], preferred_element_type=F32)
    x1_ref[...] = x1
    h2 = _rms(x1, g2_ref[...])
    h2b = h2.astype(BF16)
    h2_ref[...] = h2b
    lg_ref[...] = jnp.dot(h2b, wr_ref[...], preferred_element_type=F32) + br_ref[...]


def _merge_call(x2d, oa, ob, oc, lw):
    t, d = x2d.shape
    tm = TM_MERGE
    const = lambda *shape: pl.BlockSpec(shape, lambda i: (0,) * len(shape))
    tok = lambda w: pl.BlockSpec((tm, w), lambda i: (i, 0))
    return pl.pallas_call(
        _merge_kernel, grid=(t // tm,),
        in_specs=[tok(d), tok(512), tok(512), tok(512), const(1, d), const(d, N_BRANCHES * d),
                  const(512, d), const(512, d), const(512, d), const(d, d), const(1, d),
                  const(d, ROUTER_COLS), const(1, ROUTER_COLS)],
        out_specs=[tok(d), tok(d), tok(ROUTER_COLS)],
        out_shape=[jax.ShapeDtypeStruct((t, d), F32), jax.ShapeDtypeStruct((t, d), BF16),
                   jax.ShapeDtypeStruct((t, ROUTER_COLS), F32)],
        compiler_params=pltpu.CompilerParams(dimension_semantics=("parallel",),
                                             vmem_limit_bytes=VMEM_LIMIT),
        name="merge",
    )(x2d, oa, ob, oc, lw["ln1_g"], lw["w_gate3"], lw["w_br_a"], lw["w_br_b"], lw["w_br_c"], lw["w_out"],
      lw["ln2_g"], lw["w_router"], lw["b_router"])


def _combine_weights(lg):
    lane = lax.broadcasted_iota(jnp.int32, lg.shape, 1)
    big = jnp.int32(1 << 20)
    ninf = jnp.float32(-jnp.inf)
    is_g = lane < N_GROUPS
    gl = jnp.where(is_g, lg, ninf)
    gmax = jnp.max(gl, axis=-1, keepdims=True)
    grp = jnp.min(jnp.where(is_g & (lg == gmax), lane, big), axis=-1, keepdims=True)
    gsum = jnp.sum(jnp.where(is_g, jnp.exp(gl - gmax), 0.0), axis=-1, keepdims=True)
    grp_w = 1.0 / gsum
    lo = EXPERT_LANE0 + grp * EXP_PER_GROUP
    in_grp = (lane >= lo) & (lane < lo + EXP_PER_GROUP)
    v1 = jnp.max(jnp.where(in_grp, lg, ninf), axis=-1, keepdims=True)
    i1 = jnp.min(jnp.where(in_grp & (lg == v1), lane, big), axis=-1, keepdims=True)
    rest = in_grp & (lane != i1)
    v2 = jnp.max(jnp.where(rest, lg, ninf), axis=-1, keepdims=True)
    i2 = jnp.min(jnp.where(rest & (lg == v2), lane, big), axis=-1, keepdims=True)
    e21 = jnp.exp(v2 - v1)
    den = 1.0 + e21
    w1 = (1.0 / den) * grp_w
    w2 = (e21 / den) * grp_w
    return jnp.where(lane == i1, w1, 0.0) + jnp.where(lane == i2, w2, 0.0)


def _moe_kernel(x1_ref, h2_ref, lg_ref, wgu_ref, wd_ref, gf_ref, o_ref, a_ref, *, final_norm):
    t = h2_ref[...]
    comb = _combine_weights(lg_ref[...])
    for e in range(N_EXPERTS):
        gu = jnp.dot(t, wgu_ref[e], preferred_element_type=F32)
        gt = gu[:, 0:D_EXPERT]
        a = gt * jax.nn.sigmoid(gt) * gu[:, D_EXPERT:2 * D_EXPERT]
        ce = comb[:, EXPERT_LANE0 + e:EXPERT_LANE0 + e + 1]
        a_ref[:, e * D_EXPERT:(e + 1) * D_EXPERT] = (a * ce).astype(BF16)
    y = x1_ref[...] + jnp.dot(a_ref[...], wd_ref[...], preferred_element_type=F32)
    if final_norm:
        y = _rms(y, gf_ref[...])
    o_ref[...] = y


def _moe_call(x1, h2, lg, lw, final_g, final_norm):
    t, d = x1.shape
    tm = TM_MOE
    tok = lambda w: pl.BlockSpec((tm, w), lambda i: (i, 0))
    once = pl.Buffered(1)
    return pl.pallas_call(
        functools.partial(_moe_kernel, final_norm=final_norm), grid=(t // tm,),
        in_specs=[tok(d), tok(d), tok(ROUTER_COLS),
                  pl.BlockSpec((N_EXPERTS, d, 2 * D_EXPERT), lambda i: (0, 0, 0), pipeline_mode=once),
                  pl.BlockSpec((N_EXPERTS * D_EXPERT, d), lambda i: (0, 0), pipeline_mode=once),
                  pl.BlockSpec((1, d), lambda i: (0, 0))],
        out_specs=tok(d),
        out_shape=jax.ShapeDtypeStruct((t, d), F32),
        scratch_shapes=[pltpu.VMEM((tm, N_EXPERTS * D_EXPERT), BF16)],
        compiler_params=pltpu.CompilerParams(dimension_semantics=("parallel",),
                                             vmem_limit_bytes=VMEM_LIMIT),
        name="moe",
    )(x1, h2, lg, lw["w_gu"], lw["w_dn"], final_g)


def _layer_weights(l, w_in, ln1_g, sink_a, q_norm_g, w_uq, kv_norm_g, w_ukv, q_norm_c, k_norm_c,
                   w_br_a, w_br_b, w_br_c, w_out, ln2_g, w_grp, b_grp, w_exr, b_exr, w_gate, w_up, w_down):
    wi = w_in[l]
    d = wi.shape[0]
    z = lambda n: jnp.zeros((d, n), F32)
    w_att = jnp.concatenate([
        wi[:, 0:1152], z(64), wi[:, 1152:1184], z(32), wi[:, 1184:1696],
        wi[:, 1696:1760], z(64), wi[:, 1760:1824], z(64), wi[:, 1824:1952]], axis=1).astype(BF16)
    wq = w_uq[l].reshape(Q_LORA, HB, NOPE_B + ROPE_B)
    wq = jnp.concatenate([wq, jnp.zeros((Q_LORA, HB, HEAD_PAD - NOPE_B - ROPE_B), F32)], axis=-1)
    wkv = w_ukv[l].reshape(KV_LORA, HB, NOPE_B + V_B)
    wk = jnp.concatenate([wkv[..., :NOPE_B], jnp.zeros((KV_LORA, HB, HEAD_PAD - NOPE_B), F32)], axis=-1)
    zero64 = jnp.zeros((HD_C,), F32)
    w_router = jnp.concatenate([w_grp[l], w_exr[l], jnp.zeros((d, ROUTER_COLS - N_GROUPS - N_EXPERTS), F32)], 1)
    b_router = jnp.concatenate([b_grp[l], b_exr[l], jnp.zeros((ROUTER_COLS - N_GROUPS - N_EXPERTS,), F32)])
    return dict(
        ln1_g=ln1_g[l].reshape(1, d), w_att=w_att, w_gate3=wi[:, 1952:].astype(BF16),
        sink=sink_a[l], q_norm_g=q_norm_g[l].reshape(1, Q_LORA),
        w_uq=wq.reshape(Q_LORA, HB * HEAD_PAD).astype(BF16),
        kv_norm_g=kv_norm_g[l].reshape(1, KV_LORA),
        w_uk=wk.reshape(KV_LORA, HB * HEAD_PAD).astype(BF16),
        w_uv=wkv[..., NOPE_B:].reshape(KV_LORA, HB * V_B).astype(BF16),
        gqc=jnp.tile(q_norm_c[l], HC_Q).reshape(1, 512),
        gkc=jnp.concatenate([k_norm_c[l], zero64, k_norm_c[l], zero64]).reshape(1, 256),
        w_br_a=w_br_a[l].astype(BF16), w_br_b=w_br_b[l].astype(BF16), w_br_c=w_br_c[l].astype(BF16),
        w_out=w_out[l].astype(BF16), ln2_g=ln2_g[l].reshape(1, d),
        w_router=w_router.astype(BF16), b_router=b_router.reshape(1, ROUTER_COLS),
        w_gu=jnp.concatenate([w_gate[l], w_up[l]], axis=-1).astype(BF16),
        w_dn=w_down[l].reshape(N_EXPERTS * D_EXPERT, d).astype(BF16),
    )


def _t5_bucket(rel):
    nb = N_BUCKETS // 2
    bucket = jnp.where(rel > 0, nb, 0)
    n = jnp.abs(rel)
    max_exact = nb // 2
    nf = jnp.maximum(n, 1).astype(F32)
    large = max_exact + (jnp.log(nf / max_exact) / math.log(MAX_DISTANCE / max_exact)
                         * (nb - max_exact)).astype(jnp.int32)
    large = jnp.minimum(large, nb - 1)
    return bucket + jnp.where(n < max_exact, n, large)


def _tables(s, rel_bias):
    half = ROPE_B // 2
    inv = ROPE_THETA ** (-jnp.arange(half, dtype=F32) / half)
    pos = jnp.arange(s, dtype=jnp.int32)
    row = (pos // GRID_W).astype(F32)
    col = (pos % GRID_W).astype(F32)

    def cs(p):
        ang = p[:, None] * inv[None, :]
        return jnp.cos(ang), jnp.sin(ang)

    cp, sp = cs(pos.astype(F32))
    cr, sr = cs(row)
    cc, sc = cs(col)
    one = jnp.ones((s, 64), F32)
    z32 = jnp.zeros((s, 32), F32)
    z64 = jnp.zeros((s, 64), F32)
    b_cos = jnp.concatenate([one, cp, cp, z32], axis=1)
    b_sin = jnp.concatenate([z64, -sp, sp, z32], axis=1)
    c_cos = jnp.concatenate([cr, cr, cc, cc], axis=1)
    c_sin = jnp.concatenate([-sr, sr, -sc, sc], axis=1)
    scale_b = (NOPE_B + ROPE_B) ** -0.5 * LOG2E
    scale_c = HD_C ** -0.5 * LOG2E
    gidx = jnp.arange(512) // HD_C
    gmat = (gidx[:, None] == gidx[None, :]).astype(BF16)
    qi = jnp.arange(BLOCK)[:, None]
    kj = jnp.arange(3 * BLOCK)[None, :]
    mats = []
    for shift in (0, BLOCK, 2 * BLOCK):
        rel = kj - shift - qi
        bias = rel_bias[_t5_bucket(rel)].astype(F32)
        bias = jnp.where((jnp.abs(rel) <= WINDOW)[:, :, None], bias, NEG_INF)
        mats.append(jnp.transpose(bias, (2, 0, 1)))
    return dict(
        bqc=b_cos * scale_b, bqs=b_sin * scale_b, bkc=b_cos, bks=b_sin,
        cqc=jnp.concatenate([c_cos, c_cos], 1) * scale_c, cqs=jnp.concatenate([c_sin, c_sin], 1) * scale_c,
        ckc=jnp.concatenate([c_cos, z64], 1), cks=jnp.concatenate([c_sin, z64], 1),
        gmat=gmat, bias3=jnp.stack(mats, axis=0))


def _trunk(x, layers, tabs, final_g):
    b, s, d = x.shape
    n_layers = len(layers)
    for l, lw in enumerate(layers):
        qa, ka, va, qbt, kb, vbt, qct, kc, vct = _prep_call(x, lw, tabs)
        oa = _window_call(qa, ka, va, lw["sink"], tabs["bias3"])
        ob = _dense_call(qbt, kb, vbt, n_heads=HB, heads_per_kv=1, dq=HEAD_PAD, dk=HEAD_PAD,
                         k_stride=HEAD_PAD, name="latent_attn")
        oc = _dense_call(qct, kc, vct, n_heads=HC_Q, heads_per_kv=HC_Q // HC_KV, dq=HD_C, dk=HD_C,
                         k_stride=LANE, name="axial_attn")
        t = b * s
        x1, h2, lg = _merge_call(x.reshape(t, d), oa.reshape(t, 512), ob.reshape(t, 512),
                                 oc.reshape(t, 512), lw)
        x = _moe_call(x1, h2, lg, lw, final_g, l == n_layers - 1).reshape(b, s, d)
    return x


def kernel(x_prompt, x_sample, rel_bias, final_g, ln1_g, w_in, sink_a, q_norm_g, w_uq, kv_norm_g, w_ukv,
           q_norm_c, k_norm_c, w_br_a, w_br_b, w_br_c, w_out, ln2_g, w_grp, b_grp, w_exr, b_exr, w_gate,
           w_up, w_down):
    n_layers = w_in.shape[0]
    layers = [_layer_weights(l, w_in, ln1_g, sink_a, q_norm_g, w_uq, kv_norm_g, w_ukv, q_norm_c, k_norm_c,
                             w_br_a, w_br_b, w_br_c, w_out, ln2_g, w_grp, b_grp, w_exr, b_exr, w_gate,
                             w_up, w_down) for l in range(n_layers)]
    fg = final_g.reshape(1, -1)
    outs = []
    for x in (x_prompt, x_sample):
        tabs = _tables(x.shape[1], rel_bias)
        outs.append(_trunk(x, layers, tabs, fg))
    return tuple(outs)
```

```python
import functools
import math

import jax
import jax.numpy as jnp
from jax import lax
from jax.experimental import pallas as pl
from jax.experimental.pallas import tpu as pltpu

F32 = jnp.float32
BF16 = jnp.bfloat16

D_MODEL = 1024
GRID_W = 64
BLOCK = 128
EPS = 1e-6
NEG_INF = -1e30
ROPE_THETA = 10000.0
HA_Q, HA_KV, HD_A = 8, 2, 64
WINDOW = 128
N_BUCKETS = 32
MAX_DISTANCE = 128
HB, Q_LORA, KV_LORA, NOPE_B, ROPE_B, V_B = 8, 256, 128, 64, 32, 64
HC_Q, HC_KV, HD_C = 8, 2, 64
N_BRANCHES = 3
N_GROUPS, EXP_PER_GROUP = 4, 4
N_EXPERTS = N_GROUPS * EXP_PER_GROUP
D_EXPERT = 256

LANE = 128
HEAD_PAD = 128
ATT_COLS = 2176
ROUTER_COLS = 128
GROUP_LANE0 = 0
EXPERT_LANE0 = N_GROUPS

TM_PREP = 256
TK_DENSE = TM_PREP
TQ_DENSE = 512
TQ_WINDOW = 256
TK_WINDOW = TQ_WINDOW + 2 * WINDOW
CHUNKS_PER_TRIP = 2
SCORE_LOOKAHEAD = 2
ACC_ROWS = 80
LOG2E = math.log2(math.e)
ONES_LANE_B = NOPE_B + ROPE_B
ONES_LANE_C = HD_C
BOUND_SLACK = 1.02
SHIFT_BOUND_MAX = 48.0
TM_MERGE = 256
TM_MOE = 512
VMEM_LIMIT = 56 * 1024 * 1024


def _rms(x, g):
    return x * lax.rsqrt(jnp.mean(x * x, axis=-1, keepdims=True) + EPS) * g


def _swap16(x):
    n = x.shape[-1]
    lane = lax.broadcasted_iota(jnp.int32, x.shape, x.ndim - 1)
    return jnp.where((lane & 16) == 0, pltpu.roll(x, n - 16, x.ndim - 1), pltpu.roll(x, 16, x.ndim - 1))


def _tile_lanes(t, reps):
    return jnp.concatenate([t] * reps, axis=-1) if reps > 1 else t


def _group_ssq(x, g_ref):
    sq = x * x
    hi = sq.astype(BF16)
    lo = (sq - hi.astype(F32)).astype(BF16)
    g = g_ref[...]
    return (jnp.dot(hi, g, preferred_element_type=F32) + jnp.dot(lo, g, preferred_element_type=F32))


def _max_sq_norm(k, g_ref):
    kr = k.astype(BF16).astype(F32)
    sq = jnp.dot((kr * kr).astype(BF16), g_ref[...], preferred_element_type=F32)
    return jnp.max(sq, axis=0, keepdims=True)


def _prep_kernel(x_ref, g1_ref, w_ref, qng_ref, wuq_ref, kvng_ref, wuk_ref, wuv_ref,
                 gqc_ref, gkc_ref, gmat_ref, ghb_ref, ghc_ref, oneb_ref, onec_ref,
                 bqc_ref, bqs_ref, bkc_ref, bks_ref, cqc_ref, cqs_ref, ckc_ref, cks_ref,
                 qa_ref, ka_ref, va_ref, qbt_ref, kb_ref, vbt_ref, qct_ref, kc_ref, vct_ref,
                 kbn_ref, kcn_ref):
    x = x_ref[0]
    hb = _rms(x, g1_ref[...]).astype(BF16)

    pa = jnp.dot(hb, w_ref[:, 0:768], preferred_element_type=F32)
    qa_ref[0] = (pa[:, 0:512] * (HD_A ** -0.5 * LOG2E)).T.astype(BF16)
    ka_ref[0] = pa[:, 512:640].astype(BF16)
    vat = pa[:, 640:768].T.astype(BF16)
    for r in range(TM_PREP // BLOCK):
        va_ref[0, r] = vat[:, r * BLOCK:(r + 1) * BLOCK]

    pb = jnp.dot(hb, w_ref[:, 768:1280], preferred_element_type=F32)
    cq = _rms(pb[:, 0:256], qng_ref[...]).astype(BF16)
    qb = jnp.dot(cq, wuq_ref[...], preferred_element_type=F32)
    qb = qb * _tile_lanes(bqc_ref[...], HB) + _swap16(qb) * _tile_lanes(bqs_ref[...], HB)
    qbt_ref[0] = qb.T.astype(BF16)
    ckv = _rms(pb[:, 256:384], kvng_ref[...]).astype(BF16)
    kpe = pb[:, 384:512]
    kpe = kpe * bkc_ref[...] + _swap16(kpe) * bks_ref[...]
    kb = jnp.dot(ckv, wuk_ref[...], preferred_element_type=F32) + _tile_lanes(kpe, HB)
    kbn_ref[0, 0] = _max_sq_norm(kb, ghb_ref)
    kb_ref[0] = (kb + oneb_ref[...]).astype(BF16)
    vb = jnp.dot(ckv, wuv_ref[...], preferred_element_type=F32)
    vbt_ref[0, 0] = vb.T.astype(BF16)

    pc = jnp.dot(hb, w_ref[:, 1280:2176], preferred_element_type=F32)
    qc = pc[:, 0:512]
    qc = qc * lax.rsqrt(_group_ssq(qc, gmat_ref) * (1.0 / HD_C) + EPS) * gqc_ref[...]
    qc = qc * _tile_lanes(cqc_ref[...], 4) + _swap16(qc) * _tile_lanes(cqs_ref[...], 4)
    qct_ref[0] = qc.T.astype(BF16)
    kc = pc[:, 512:768]
    kc = kc * lax.rsqrt(_group_ssq(kc, gmat_ref.at[0:256, 0:256]) * (1.0 / HD_C) + EPS) * gkc_ref[...]
    kc = kc * _tile_lanes(ckc_ref[...], 2) + _swap16(kc) * _tile_lanes(cks_ref[...], 2)
    kcn_ref[0, 0] = _max_sq_norm(kc, ghc_ref)
    kc_ref[0] = (kc + onec_ref[...]).astype(BF16)
    vct_ref[0, 0] = pc[:, 768:896].T.astype(BF16)


def _prep_call(x, lw, tabs):
    b, s, d = x.shape
    tm = TM_PREP
    nt = s // tm
    const = lambda *shape: pl.BlockSpec(shape, lambda bi, ti: (0,) * len(shape))
    tab = lambda w: pl.BlockSpec((tm, w), lambda bi, ti: (ti, 0))
    in_specs = [
        pl.BlockSpec((1, tm, d), lambda bi, ti: (bi, ti, 0)),
        const(1, d), const(d, ATT_COLS), const(1, Q_LORA), const(Q_LORA, HB * HEAD_PAD),
        const(1, KV_LORA), const(KV_LORA, HB * HEAD_PAD), const(KV_LORA, HB * V_B),
        const(1, 512), const(1, 256), const(512, 512),
        const(HB * HEAD_PAD, LANE), const(256, LANE), const(1, HB * HEAD_PAD), const(1, 256),
        tab(LANE), tab(LANE), tab(LANE), tab(LANE), tab(LANE), tab(LANE), tab(LANE), tab(LANE),
    ]
    tok = lambda w: pl.BlockSpec((1, tm, w), lambda bi, ti: (bi, ti, 0))
    tr = lambda w: pl.BlockSpec((1, w, tm), lambda bi, ti: (bi, 0, ti))
    chunked = lambda w: pl.BlockSpec((1, 1, w, tm), lambda bi, ti: (bi, ti, 0, 0))
    blocks = pl.BlockSpec((1, tm // BLOCK, 128, BLOCK), lambda bi, ti: (bi, ti, 0, 0))
    tile_row = pl.BlockSpec((1, 1, 1, LANE), lambda bi, ti: (bi, ti, 0, 0))
    out_specs = [tr(512), tok(128), blocks, tr(HB * HEAD_PAD), tok(HB * HEAD_PAD), chunked(HB * V_B),
                 tr(512), tok(256), chunked(128), tile_row, tile_row]
    out_shape = [
        jax.ShapeDtypeStruct((b, 512, s), BF16), jax.ShapeDtypeStruct((b, s, 128), BF16),
        jax.ShapeDtypeStruct((b, s // BLOCK, 128, BLOCK), BF16),
        jax.ShapeDtypeStruct((b, HB * HEAD_PAD, s), BF16), jax.ShapeDtypeStruct((b, s, HB * HEAD_PAD), BF16),
        jax.ShapeDtypeStruct((b, nt, HB * V_B, tm), BF16),
        jax.ShapeDtypeStruct((b, 512, s), BF16), jax.ShapeDtypeStruct((b, s, 256), BF16),
        jax.ShapeDtypeStruct((b, nt, 128, tm), BF16),
        jax.ShapeDtypeStruct((b, nt, 1, LANE), F32), jax.ShapeDtypeStruct((b, nt, 1, LANE), F32),
    ]
    return pl.pallas_call(
        _prep_kernel, grid=(b, nt), in_specs=in_specs, out_specs=out_specs, out_shape=out_shape,
        compiler_params=pltpu.CompilerParams(dimension_semantics=("parallel", "parallel"),
                                             vmem_limit_bytes=VMEM_LIMIT),
        name="prep",
    )(x, lw["ln1_g"], lw["w_att"], lw["q_norm_g"], lw["w_uq"], lw["kv_norm_g"], lw["w_uk"], lw["w_uv"],
      lw["gqc"], lw["gkc"], tabs["gmat"], tabs["ghb"], tabs["ghc"], tabs["oneb"], tabs["onec"],
      tabs["bqc"], tabs["bqs"], tabs["bkc"], tabs["bks"], tabs["cqc"], tabs["cqs"], tabs["ckc"], tabs["cks"])


def _window_kernel(sink_ref, qt_ref, k_ref, vt_ref, bias_ref, o_ref, ot_ref):
    nblk = vt_ref.shape[1]
    wblk = TK_WINDOW // BLOCK
    i = pl.program_id(1)
    cs = jnp.clip(i * (TQ_WINDOW // BLOCK) - 1, 0, nblk - wblk)
    start = pl.multiple_of(cs * BLOCK, BLOCK)
    kwin = k_ref[0, pl.ds(start, TK_WINDOW), :]
    grp = HA_Q // HA_KV
    ones_rows = (lax.broadcasted_iota(jnp.int32, (ACC_ROWS - HD_A, TK_WINDOW), 0) == 0).astype(BF16)
    kgs, vexts = [], []
    for g in range(HA_KV):
        kgs.append(kwin[:, g * HD_A:(g + 1) * HD_A])
        vwin = jnp.concatenate([vt_ref[0, cs + r, g * HD_A:(g + 1) * HD_A, :] for r in range(wblk)], axis=1)
        vexts.append(jnp.concatenate([vwin, ones_rows], axis=0))

    npair = HA_Q // 2

    def scores(c):
        qt = jnp.concatenate([qt_ref[0, h * HD_A:(h + 1) * HD_A, :] for h in (2 * c, 2 * c + 1)], axis=1)
        return jnp.dot(kgs[2 * c // grp], qt, preferred_element_type=F32)

    pending = [scores(c) for c in range(SCORE_LOOKAHEAD)]
    for c in range(npair):
        st = pending.pop(0)
        if c + SCORE_LOOKAHEAD < npair:
            pending.append(scores(c + SCORE_LOOKAHEAD))
        st = st + bias_ref[0, c]
        sink = sink_ref[c]
        m = jnp.maximum(jnp.max(st, axis=0, keepdims=True), sink)
        p = jnp.exp2(st - m).astype(BF16)
        pv = jnp.dot(vexts[2 * c // grp], p, preferred_element_type=F32)
        o2 = pv[0:HD_A, :] / (pv[HD_A:HD_A + 1, :] + jnp.exp2(sink - m))
        for r in range(2):
            h = 2 * c + r
            ot_ref[h * HD_A:(h + 1) * HD_A, :] = o2[:, r * TQ_WINDOW:(r + 1) * TQ_WINDOW]
    o_ref[0] = ot_ref[...].T.astype(BF16)


def _window_call(qat, ka, vat, sink, bias3):
    b, _, s = qat.shape
    nq = s // TQ_WINDOW
    variant = lambda bi, i: (jnp.where(i == 0, 0, jnp.where(i == nq - 1, 2, 1)), 0, 0, 0)
    return pl.pallas_call(
        _window_kernel, grid=(b, nq),
        in_specs=[
            pl.BlockSpec((HA_Q // 2, 1, 2 * TQ_WINDOW), lambda bi, i: (0, 0, 0)),
            pl.BlockSpec((1, 512, TQ_WINDOW), lambda bi, i: (bi, 0, i)),
            pl.BlockSpec((1, s, 128), lambda bi, i: (bi, 0, 0)),
            pl.BlockSpec((1, s // BLOCK, 128, BLOCK), lambda bi, i: (bi, 0, 0, 0)),
            pl.BlockSpec((1, HA_Q // 2, TK_WINDOW, 2 * TQ_WINDOW), variant),
        ],
        out_specs=pl.BlockSpec((1, TQ_WINDOW, 512), lambda bi, i: (bi, i, 0)),
        out_shape=jax.ShapeDtypeStruct((b, s, 512), BF16),
        scratch_shapes=[pltpu.VMEM((HA_Q * HD_A, TQ_WINDOW), F32)],
        compiler_params=pltpu.CompilerParams(dimension_semantics=("parallel", "arbitrary"),
                                             vmem_limit_bytes=VMEM_LIMIT),
        name="window_attn",
    )(sink, qat, ka, vat, bias3)


def _dense_kernel(qt_ref, k_ref, vt_ref, kn_ref, o_ref, acc_ref, ot_ref, st_ref, qs_ref, *, n_heads,
                  heads_per_kv, dq, dk, k_stride, ones_lane):
    nchunk = vt_ref.shape[1]
    tkc = vt_ref.shape[3]
    tq = qt_ref.shape[2]
    tk = CHUNKS_PER_TRIP * tkc
    ntrip = nchunk // CHUNKS_PER_TRIP
    ones_rows = (lax.broadcasted_iota(jnp.int32, (ACC_ROWS - 64, tkc), 0) == 0).astype(BF16)

    kn2 = jnp.max(kn_ref[0, :, 0, :], axis=0, keepdims=True)
    lane = lax.broadcasted_iota(jnp.int32, kn2.shape, 1)
    row = lax.broadcasted_iota(jnp.int32, (HEAD_PAD, tq), 0)
    bound = None
    for h in range(n_heads):
        g = h // heads_per_kv
        q = qt_ref[0, h * dq:h * dq + dk, :]
        if dk < HEAD_PAD:
            q = jnp.concatenate([q, jnp.zeros((HEAD_PAD - dk, tq), BF16)], axis=0)
        qf = q.astype(F32)
        kmax2 = jnp.sum(jnp.where(lane == g, kn2, 0.0), axis=1, keepdims=True)
        u = jnp.sqrt(jnp.sum(qf * qf, axis=0, keepdims=True) * kmax2) * BOUND_SLACK
        qs_ref[h * HEAD_PAD:(h + 1) * HEAD_PAD, :] = jnp.where(row == ones_lane, -u, qf).astype(BF16)
        hmax = jnp.max(u)
        bound = hmax if bound is None else jnp.maximum(bound, hmax)
    shifted_ok = bound <= SHIFT_BOUND_MAX

    def scores(t, h, shifted):
        g = h // heads_per_kv
        off = pl.multiple_of(t * tk, tk)
        if shifted:
            qt = qs_ref[h * HEAD_PAD:(h + 1) * HEAD_PAD, :]
            kc = k_ref[0, pl.ds(off, tk), g * k_stride:g * k_stride + HEAD_PAD]
        else:
            qt = qt_ref[0, h * dq:h * dq + dk, :]
            kc = k_ref[0, pl.ds(off, tk), g * k_stride:g * k_stride + dk]
        return jnp.dot(kc, qt, preferred_element_type=F32)

    def attend(shifted):
        acc_ref[...] = jnp.zeros(acc_ref.shape, F32)
        for j in range(SCORE_LOOKAHEAD):
            st_ref[j] = scores(0, j, shifted)

        def body(t, ms):
            t_next = jnp.minimum(t + 1, ntrip - 1)
            pending = [st_ref[j] for j in range(SCORE_LOOKAHEAD)]
            new_ms = []
            for h in range(n_heads):
                g = h // heads_per_kv
                st = pending.pop(0)
                ahead = h + SCORE_LOOKAHEAD
                if ahead < n_heads:
                    pending.append(scores(t, ahead, shifted))
                else:
                    st_ref[ahead - n_heads] = scores(t_next, ahead - n_heads, shifted)
                if shifted:
                    p = jnp.exp2(st).astype(BF16)
                else:
                    m_new = jnp.maximum(ms[h], jnp.max(st, axis=0, keepdims=True))
                    alpha = jnp.exp2(ms[h] - m_new)
                    p = jnp.exp2(st - m_new).astype(BF16)
                    new_ms.append(m_new)
                pv = None
                for cc in range(CHUNKS_PER_TRIP):
                    vc = vt_ref[0, t * CHUNKS_PER_TRIP + cc, g * 64:(g + 1) * 64, :]
                    part = jnp.dot(jnp.concatenate([vc, ones_rows], axis=0), p[cc * tkc:(cc + 1) * tkc, :],
                                   preferred_element_type=F32)
                    pv = part if pv is None else pv + part
                rows = slice(h * ACC_ROWS, (h + 1) * ACC_ROWS)
                if shifted:
                    acc_ref[rows, :] = acc_ref[rows, :] + pv
                else:
                    acc_ref[rows, :] = alpha * acc_ref[rows, :] + pv
            return tuple(new_ms)

        init = () if shifted else (jnp.full((1, tq), -jnp.inf, F32),) * n_heads
        lax.fori_loop(0, ntrip, body, init)

    pl.when(shifted_ok)(functools.partial(attend, True))
    pl.when(jnp.logical_not(shifted_ok))(functools.partial(attend, False))

    for h in range(n_heads):
        num = acc_ref[h * ACC_ROWS:h * ACC_ROWS + 64, :]
        den = acc_ref[h * ACC_ROWS + 64:h * ACC_ROWS + 65, :]
        ot_ref[h * 64:(h + 1) * 64, :] = num / den
    o_ref[0] = ot_ref[...].T.astype(BF16)


def _dense_call(qt, k, vt, kn, *, n_heads, heads_per_kv, dq, dk, k_stride, ones_lane, name):
    b, fq, s = qt.shape
    tq = TQ_DENSE
    fk = k.shape[2]
    _, nchunk, fv, tk = vt.shape
    nt = kn.shape[1]
    kern = functools.partial(_dense_kernel, n_heads=n_heads, heads_per_kv=heads_per_kv, dq=dq, dk=dk,
                             k_stride=k_stride, ones_lane=ones_lane)
    return pl.pallas_call(
        kern, grid=(b, s // tq),
        in_specs=[
            pl.BlockSpec((1, fq, tq), lambda bi, i: (bi, 0, i)),
            pl.BlockSpec((1, s, fk), lambda bi, i: (bi, 0, 0)),
            pl.BlockSpec((1, nchunk, fv, tk), lambda bi, i: (bi, 0, 0, 0)),
            pl.BlockSpec((1, nt, 1, LANE), lambda bi, i: (bi, 0, 0, 0)),
        ],
        out_specs=pl.BlockSpec((1, tq, n_heads * 64), lambda bi, i: (bi, i, 0)),
        out_shape=jax.ShapeDtypeStruct((b, s, n_heads * 64), BF16),
        scratch_shapes=[pltpu.VMEM((n_heads * ACC_ROWS, tq), F32), pltpu.VMEM((n_heads * 64, tq), F32),
                        pltpu.VMEM((SCORE_LOOKAHEAD, CHUNKS_PER_TRIP * tk, tq), F32),
                        pltpu.VMEM((n_heads * HEAD_PAD, tq), BF16)],
        compiler_params=pltpu.CompilerParams(dimension_semantics=("parallel", "arbitrary"),
                                             vmem_limit_bytes=VMEM_LIMIT),
        name=name,
    )(qt, k, vt, kn)


def _merge_kernel(x_ref, oa_ref, ob_ref, oc_ref, g1_ref, wg_ref, wa_ref, wb_ref, wc_ref, wo_ref,
                  g2_ref, wr_ref, br_ref, x1_ref, h2_ref, lg_ref):
    x = x_ref[...]
    hb = _rms(x, g1_ref[...]).astype(BF16)
    merged = None
    for j, (o_ref, w_ref) in enumerate(((oa_ref, wa_ref), (ob_ref, wb_ref), (oc_ref, wc_ref))):
        gl = jnp.dot(hb, wg_ref[:, j * D_MODEL:(j + 1) * D_MODEL], preferred_element_type=F32)
        y = jnp.dot(o_ref[...], w_ref[...], preferred_element_type=F32)
        t = jax.nn.sigmoid(gl) * y
        merged = t if merged is None else merged + t
    x1 = x + jnp.dot(merged.astype(BF16), wo_ref[...], preferred_element_type=F32)
    x1_ref[...] = x1
    h2 = _rms(x1, g2_ref[...])
    h2b = h2.astype(BF16)
    h2_ref[...] = h2b
    lg_ref[...] = jnp.dot(h2b, wr_ref[...], preferred_element_type=F32) + br_ref[...]


def _merge_call(x2d, oa, ob, oc, lw):
    t, d = x2d.shape
    tm = TM_MERGE
    const = lambda *shape: pl.BlockSpec(shape, lambda i: (0,) * len(shape))
    tok = lambda w: pl.BlockSpec((tm, w), lambda i: (i, 0))
    return pl.pallas_call(
        _merge_kernel, grid=(t // tm,),
        in_specs=[tok(d), tok(512), tok(512), tok(512), const(1, d), const(d, N_BRANCHES * d),
                  const(512, d), const(512, d), const(512, d), const(d, d), const(1, d),
                  const(d, ROUTER_COLS), const(1, ROUTER_COLS)],
        out_specs=[tok(d), tok(d), tok(ROUTER_COLS)],
        out_shape=[jax.ShapeDtypeStruct((t, d), F32), jax.ShapeDtypeStruct((t, d), BF16),
                   jax.ShapeDtypeStruct((t, ROUTER_COLS), F32)],
        compiler_params=pltpu.CompilerParams(dimension_semantics=("parallel",),
                                             vmem_limit_bytes=VMEM_LIMIT),
        name="merge",
    )(x2d, oa, ob, oc, lw["ln1_g"], lw["w_gate3"], lw["w_br_a"], lw["w_br_b"], lw["w_br_c"], lw["w_out"],
      lw["ln2_g"], lw["w_router"], lw["b_router"])


def _combine_weights(lg):
    lane = lax.broadcasted_iota(jnp.int32, lg.shape, 1)
    big = jnp.int32(1 << 20)
    ninf = jnp.float32(-jnp.inf)
    is_g = lane < N_GROUPS
    gl = jnp.where(is_g, lg, ninf)
    gmax = jnp.max(gl, axis=-1, keepdims=True)
    grp = jnp.min(jnp.where(is_g & (lg == gmax), lane, big), axis=-1, keepdims=True)
    gsum = jnp.sum(jnp.where(is_g, jnp.exp(gl - gmax), 0.0), axis=-1, keepdims=True)
    grp_w = 1.0 / gsum
    lo = EXPERT_LANE0 + grp * EXP_PER_GROUP
    in_grp = (lane >= lo) & (lane < lo + EXP_PER_GROUP)
    v1 = jnp.max(jnp.where(in_grp, lg, ninf), axis=-1, keepdims=True)
    i1 = jnp.min(jnp.where(in_grp & (lg == v1), lane, big), axis=-1, keepdims=True)
    rest = in_grp & (lane != i1)
    v2 = jnp.max(jnp.where(rest, lg, ninf), axis=-1, keepdims=True)
    i2 = jnp.min(jnp.where(rest & (lg == v2), lane, big), axis=-1, keepdims=True)
    e21 = jnp.exp(v2 - v1)
    den = 1.0 + e21
    w1 = (1.0 / den) * grp_w
    w2 = (e21 / den) * grp_w
    return jnp.where(lane == i1, w1, 0.0) + jnp.where(lane == i2, w2, 0.0)


def _moe_kernel(x1_ref, h2_ref, lg_ref, wgu_ref, wd_ref, gf_ref, o_ref, a_ref, *, final_norm):
    t = h2_ref[...]
    comb = _combine_weights(lg_ref[...])
    for e in range(N_EXPERTS):
        gu = jnp.dot(t, wgu_ref[e], preferred_element_type=F32)
        gt = gu[:, 0:D_EXPERT]
        a = gt * jax.nn.sigmoid(gt) * gu[:, D_EXPERT:2 * D_EXPERT]
        ce = comb[:, EXPERT_LANE0 + e:EXPERT_LANE0 + e + 1]
        a_ref[:, e * D_EXPERT:(e + 1) * D_EXPERT] = (a * ce).astype(BF16)
    y = x1_ref[...] + jnp.dot(a_ref[...], wd_ref[...], preferred_element_type=F32)
    if final_norm:
        y = _rms(y, gf_ref[...])
    o_ref[...] = y


def _moe_call(x1, h2, lg, lw, final_g, final_norm):
    t, d = x1.shape
    tm = TM_MOE
    tok = lambda w: pl.BlockSpec((tm, w), lambda i: (i, 0))
    once = pl.Buffered(1)
    return pl.pallas_call(
        functools.partial(_moe_kernel, final_norm=final_norm), grid=(t // tm,),
        in_specs=[tok(d), tok(d), tok(ROUTER_COLS),
                  pl.BlockSpec((N_EXPERTS, d, 2 * D_EXPERT), lambda i: (0, 0, 0), pipeline_mode=once),
                  pl.BlockSpec((N_EXPERTS * D_EXPERT, d), lambda i: (0, 0), pipeline_mode=once),
                  pl.BlockSpec((1, d), lambda i: (0, 0))],
        out_specs=tok(d),
        out_shape=jax.ShapeDtypeStruct((t, d), F32),
        scratch_shapes=[pltpu.VMEM((tm, N_EXPERTS * D_EXPERT), BF16)],
        compiler_params=pltpu.CompilerParams(dimension_semantics=("parallel",),
                                             vmem_limit_bytes=VMEM_LIMIT),
        name="moe",
    )(x1, h2, lg, lw["w_gu"], lw["w_dn"], final_g)


def _layer_weights(l, w_in, ln1_g, sink_a, q_norm_g, w_uq, kv_norm_g, w_ukv, q_norm_c, k_norm_c,
                   w_br_a, w_br_b, w_br_c, w_out, ln2_g, w_grp, b_grp, w_exr, b_exr, w_gate, w_up, w_down):
    wi = w_in[l]
    d = wi.shape[0]
    z = lambda n: jnp.zeros((d, n), F32)
    w_att = jnp.concatenate([
        wi[:, 0:1152], z(64), wi[:, 1152:1184], z(32), wi[:, 1184:1696],
        wi[:, 1696:1760], z(64), wi[:, 1760:1824], z(64), wi[:, 1824:1952]], axis=1).astype(BF16)
    wq = w_uq[l].reshape(Q_LORA, HB, NOPE_B + ROPE_B)
    wq = jnp.concatenate([wq, jnp.zeros((Q_LORA, HB, HEAD_PAD - NOPE_B - ROPE_B), F32)], axis=-1)
    wkv = w_ukv[l].reshape(KV_LORA, HB, NOPE_B + V_B)
    wk = jnp.concatenate([wkv[..., :NOPE_B], jnp.zeros((KV_LORA, HB, HEAD_PAD - NOPE_B), F32)], axis=-1)
    zero64 = jnp.zeros((HD_C,), F32)
    w_router = jnp.concatenate([w_grp[l], w_exr[l], jnp.zeros((d, ROUTER_COLS - N_GROUPS - N_EXPERTS), F32)], 1)
    b_router = jnp.concatenate([b_grp[l], b_exr[l], jnp.zeros((ROUTER_COLS - N_GROUPS - N_EXPERTS,), F32)])
    return dict(
        ln1_g=ln1_g[l].reshape(1, d), w_att=w_att, w_gate3=wi[:, 1952:].astype(BF16),
        sink=jnp.repeat(sink_a[l] * LOG2E, TQ_WINDOW).reshape(HA_Q // 2, 1, 2 * TQ_WINDOW), q_norm_g=q_norm_g[l].reshape(1, Q_LORA),
        w_uq=wq.reshape(Q_LORA, HB * HEAD_PAD).astype(BF16),
        kv_norm_g=kv_norm_g[l].reshape(1, KV_LORA),
        w_uk=wk.reshape(KV_LORA, HB * HEAD_PAD).astype(BF16),
        w_uv=wkv[..., NOPE_B:].reshape(KV_LORA, HB * V_B).astype(BF16),
        gqc=jnp.tile(q_norm_c[l], HC_Q).reshape(1, 512),
        gkc=jnp.concatenate([k_norm_c[l], zero64, k_norm_c[l], zero64]).reshape(1, 256),
        w_br_a=w_br_a[l].astype(BF16), w_br_b=w_br_b[l].astype(BF16), w_br_c=w_br_c[l].astype(BF16),
        w_out=w_out[l].astype(BF16), ln2_g=ln2_g[l].reshape(1, d),
        w_router=w_router.astype(BF16), b_router=b_router.reshape(1, ROUTER_COLS),
        w_gu=jnp.concatenate([w_gate[l], w_up[l]], axis=-1).astype(BF16),
        w_dn=w_down[l].reshape(N_EXPERTS * D_EXPERT, d).astype(BF16),
    )


def _t5_bucket(rel):
    nb = N_BUCKETS // 2
    bucket = jnp.where(rel > 0, nb, 0)
    n = jnp.abs(rel)
    max_exact = nb // 2
    nf = jnp.maximum(n, 1).astype(F32)
    large = max_exact + (jnp.log(nf / max_exact) / math.log(MAX_DISTANCE / max_exact)
                         * (nb - max_exact)).astype(jnp.int32)
    large = jnp.minimum(large, nb - 1)
    return bucket + jnp.where(n < max_exact, n, large)


def _tables(s, rel_bias):
    half = ROPE_B // 2
    inv = ROPE_THETA ** (-jnp.arange(half, dtype=F32) / half)
    pos = jnp.arange(s, dtype=jnp.int32)
    row = (pos // GRID_W).astype(F32)
    col = (pos % GRID_W).astype(F32)

    def cs(p):
        ang = p[:, None] * inv[None, :]
        return jnp.cos(ang), jnp.sin(ang)

    cp, sp = cs(pos.astype(F32))
    cr, sr = cs(row)
    cc, sc = cs(col)
    one = jnp.ones((s, 64), F32)
    z32 = jnp.zeros((s, 32), F32)
    z64 = jnp.zeros((s, 64), F32)
    b_cos = jnp.concatenate([one, cp, cp, z32], axis=1)
    b_sin = jnp.concatenate([z64, -sp, sp, z32], axis=1)
    c_cos = jnp.concatenate([cr, cr, cc, cc], axis=1)
    c_sin = jnp.concatenate([-sr, sr, -sc, sc], axis=1)
    scale_b = (NOPE_B + ROPE_B) ** -0.5 * LOG2E
    scale_c = HD_C ** -0.5 * LOG2E
    gidx = jnp.arange(512) // HD_C
    gmat = (gidx[:, None] == gidx[None, :]).astype(BF16)
    kj = jnp.arange(TK_WINDOW)[:, None]
    qi = jnp.arange(TQ_WINDOW)[None, :]
    mats = []
    for shift in (0, WINDOW, 2 * WINDOW):
        rel = kj - shift - qi
        bias = rel_bias[_t5_bucket(rel)].astype(F32) * LOG2E
        bias = jnp.where((jnp.abs(rel) <= WINDOW)[:, :, None], bias, NEG_INF)
        bias = jnp.transpose(bias, (0, 2, 1)).reshape(TK_WINDOW, HA_Q // 2, 2 * TQ_WINDOW)
        mats.append(jnp.transpose(bias, (1, 0, 2)))
    return dict(
        bqc=b_cos * scale_b, bqs=b_sin * scale_b, bkc=b_cos, bks=b_sin,
        cqc=jnp.concatenate([c_cos, c_cos], 1) * scale_c, cqs=jnp.concatenate([c_sin, c_sin], 1) * scale_c,
        ckc=jnp.concatenate([c_cos, z64], 1), cks=jnp.concatenate([c_sin, z64], 1),
        gmat=gmat, bias3=jnp.stack(mats, axis=0), **_key_norm_constants())


def _key_norm_constants():
    lb = jnp.arange(HB * HEAD_PAD)
    ghb = ((lb[:, None] // HEAD_PAD == jnp.arange(LANE)[None, :]) & (lb[:, None] % HEAD_PAD < ONES_LANE_B))
    lc = jnp.arange(HC_KV * LANE)
    ghc = ((lc[:, None] // LANE == jnp.arange(LANE)[None, :]) & (lc[:, None] % LANE < ONES_LANE_C))
    return dict(ghb=ghb.astype(BF16), ghc=ghc.astype(BF16),
                oneb=(lb % HEAD_PAD == ONES_LANE_B).astype(F32).reshape(1, -1),
                onec=(lc % LANE == ONES_LANE_C).astype(F32).reshape(1, -1))


def _trunk(x, layers, tabs, final_g):
    b, s, d = x.shape
    n_layers = len(layers)
    for l, lw in enumerate(layers):
        qa, ka, va, qbt, kb, vbt, qct, kc, vct, kbn, kcn = _prep_call(x, lw, tabs)
        oa = _window_call(qa, ka, va, lw["sink"], tabs["bias3"])
        ob = _dense_call(qbt, kb, vbt, kbn, n_heads=HB, heads_per_kv=1, dq=HEAD_PAD, dk=HEAD_PAD,
                         k_stride=HEAD_PAD, ones_lane=ONES_LANE_B, name="latent_attn")
        oc = _dense_call(qct, kc, vct, kcn, n_heads=HC_Q, heads_per_kv=HC_Q // HC_KV, dq=HD_C, dk=HD_C,
                         k_stride=LANE, ones_lane=ONES_LANE_C, name="axial_attn")
        t = b * s
        x1, h2, lg = _merge_call(x.reshape(t, d), oa.reshape(t, 512), ob.reshape(t, 512),
                                 oc.reshape(t, 512), lw)
        x = _moe_call(x1, h2, lg, lw, final_g, l == n_layers - 1).reshape(b, s, d)
    return x


def kernel(x_prompt, x_sample, rel_bias, final_g, ln1_g, w_in, sink_a, q_norm_g, w_uq, kv_norm_g, w_ukv,
           q_norm_c, k_norm_c, w_br_a, w_br_b, w_br_c, w_out, ln2_g, w_grp, b_grp, w_exr, b_exr, w_gate,
           w_up, w_down):
    n_layers = w_in.shape[0]
    layers = [_layer_weights(l, w_in, ln1_g, sink_a, q_norm_g, w_uq, kv_norm_g, w_ukv, q_norm_c, k_norm_c,
                             w_br_a, w_br_b, w_br_c, w_out, ln2_g, w_grp, b_grp, w_exr, b_exr, w_gate,
                             w_up, w_down) for l in range(n_layers)]
    fg = final_g.reshape(1, -1)
    outs = []
    for x in (x_prompt, x_sample):
        tabs = _tables(x.shape[1], rel_bias)
        outs.append(_trunk(x, layers, tabs, fg))
    return tuple(outs)
```

```python
import functools
import math

import jax
import jax.numpy as jnp
from jax import lax
from jax.experimental import pallas as pl
from jax.experimental.pallas import tpu as pltpu

F32 = jnp.float32
BF16 = jnp.bfloat16

D_MODEL = 1024
GRID_W = 64
BLOCK = 128
EPS = 1e-6
NEG_INF = -1e30
ROPE_THETA = 10000.0
HA_Q, HA_KV, HD_A = 8, 2, 64
WINDOW = 128
N_BUCKETS = 32
MAX_DISTANCE = 128
HB, Q_LORA, KV_LORA, NOPE_B, ROPE_B, V_B = 8, 256, 128, 64, 32, 64
HC_Q, HC_KV, HD_C = 8, 2, 64
N_BRANCHES = 3
N_GROUPS, EXP_PER_GROUP = 4, 4
N_EXPERTS = N_GROUPS * EXP_PER_GROUP
D_EXPERT = 256

LANE = 128
HEAD_PAD = 128
ATT_COLS = 2176
ROUTER_COLS = 128
GROUP_LANE0 = 0
EXPERT_LANE0 = N_GROUPS

TM_PREP = 256
TK_DENSE = TM_PREP
TQ_DENSE = 512
TQ_WINDOW = 256
TK_WINDOW = TQ_WINDOW + 2 * WINDOW
CHUNKS_PER_TRIP = 2
SCORE_LOOKAHEAD = 2
ACC_ROWS = 80
LOG2E = math.log2(math.e)
ONES_LANE_B = NOPE_B + ROPE_B
ONES_LANE_C = HD_C
BOUND_SLACK = 1.02
SHIFT_BOUND_MAX = 48.0
TM_MERGE = 256
TM_MOE = 512
MOE_BLOCK = 128
VMEM_LIMIT = 56 * 1024 * 1024


def _rms(x, g):
    return x * lax.rsqrt(jnp.mean(x * x, axis=-1, keepdims=True) + EPS) * g


def _swap16(x):
    n = x.shape[-1]
    lane = lax.broadcasted_iota(jnp.int32, x.shape, x.ndim - 1)
    return jnp.where((lane & 16) == 0, pltpu.roll(x, n - 16, x.ndim - 1), pltpu.roll(x, 16, x.ndim - 1))


def _tile_lanes(t, reps):
    return jnp.concatenate([t] * reps, axis=-1) if reps > 1 else t


def _group_ssq(x, g_ref):
    sq = x * x
    hi = sq.astype(BF16)
    lo = (sq - hi.astype(F32)).astype(BF16)
    g = g_ref[...]
    return (jnp.dot(hi, g, preferred_element_type=F32) + jnp.dot(lo, g, preferred_element_type=F32))


def _max_sq_norm(k, g_ref):
    kr = k.astype(BF16).astype(F32)
    sq = jnp.dot((kr * kr).astype(BF16), g_ref[...], preferred_element_type=F32)
    return jnp.max(sq, axis=0, keepdims=True)


def _prep_kernel(x_ref, g1_ref, w_ref, qng_ref, wuq_ref, kvng_ref, wuk_ref, wuv_ref,
                 gqc_ref, gkc_ref, gmat_ref, ghb_ref, ghc_ref, oneb_ref, onec_ref,
                 bqc_ref, bqs_ref, bkc_ref, bks_ref, cqc_ref, cqs_ref, ckc_ref, cks_ref,
                 qa_ref, ka_ref, va_ref, qbt_ref, kb_ref, vbt_ref, qct_ref, kc_ref, vct_ref,
                 kbn_ref, kcn_ref):
    x = x_ref[0]
    hb = _rms(x, g1_ref[...]).astype(BF16)

    pa = jnp.dot(hb, w_ref[:, 0:768], preferred_element_type=F32)
    qa_ref[0] = (pa[:, 0:512] * (HD_A ** -0.5 * LOG2E)).T.astype(BF16)
    ka_ref[0] = pa[:, 512:640].astype(BF16)
    vat = pa[:, 640:768].T.astype(BF16)
    for r in range(TM_PREP // BLOCK):
        va_ref[0, r] = vat[:, r * BLOCK:(r + 1) * BLOCK]

    pb = jnp.dot(hb, w_ref[:, 768:1280], preferred_element_type=F32)
    cq = _rms(pb[:, 0:256], qng_ref[...]).astype(BF16)
    qb = jnp.dot(cq, wuq_ref[...], preferred_element_type=F32)
    qb = qb * _tile_lanes(bqc_ref[...], HB) + _swap16(qb) * _tile_lanes(bqs_ref[...], HB)
    qbt_ref[0] = qb.T.astype(BF16)
    ckv = _rms(pb[:, 256:384], kvng_ref[...]).astype(BF16)
    kpe = pb[:, 384:512]
    kpe = kpe * bkc_ref[...] + _swap16(kpe) * bks_ref[...]
    kb = jnp.dot(ckv, wuk_ref[...], preferred_element_type=F32) + _tile_lanes(kpe, HB)
    kbn_ref[0, 0] = _max_sq_norm(kb, ghb_ref)
    kb_ref[0] = (kb + oneb_ref[...]).astype(BF16)
    vb = jnp.dot(ckv, wuv_ref[...], preferred_element_type=F32)
    vbt_ref[0, 0] = vb.T.astype(BF16)

    pc = jnp.dot(hb, w_ref[:, 1280:2176], preferred_element_type=F32)
    qc = pc[:, 0:512]
    qc = qc * lax.rsqrt(_group_ssq(qc, gmat_ref) * (1.0 / HD_C) + EPS) * gqc_ref[...]
    qc = qc * _tile_lanes(cqc_ref[...], 4) + _swap16(qc) * _tile_lanes(cqs_ref[...], 4)
    qct_ref[0] = qc.T.astype(BF16)
    kc = pc[:, 512:768]
    kc = kc * lax.rsqrt(_group_ssq(kc, gmat_ref.at[0:256, 0:256]) * (1.0 / HD_C) + EPS) * gkc_ref[...]
    kc = kc * _tile_lanes(ckc_ref[...], 2) + _swap16(kc) * _tile_lanes(cks_ref[...], 2)
    kcn_ref[0, 0] = _max_sq_norm(kc, ghc_ref)
    kc_ref[0] = (kc + onec_ref[...]).astype(BF16)
    vct_ref[0, 0] = pc[:, 768:896].T.astype(BF16)


def _prep_call(x, lw, tabs):
    b, s, d = x.shape
    tm = TM_PREP
    nt = s // tm
    const = lambda *shape: pl.BlockSpec(shape, lambda bi, ti: (0,) * len(shape))
    tab = lambda w: pl.BlockSpec((tm, w), lambda bi, ti: (ti, 0))
    in_specs = [
        pl.BlockSpec((1, tm, d), lambda bi, ti: (bi, ti, 0)),
        const(1, d), const(d, ATT_COLS), const(1, Q_LORA), const(Q_LORA, HB * HEAD_PAD),
        const(1, KV_LORA), const(KV_LORA, HB * HEAD_PAD), const(KV_LORA, HB * V_B),
        const(1, 512), const(1, 256), const(512, 512),
        const(HB * HEAD_PAD, LANE), const(256, LANE), const(1, HB * HEAD_PAD), const(1, 256),
        tab(LANE), tab(LANE), tab(LANE), tab(LANE), tab(LANE), tab(LANE), tab(LANE), tab(LANE),
    ]
    tok = lambda w: pl.BlockSpec((1, tm, w), lambda bi, ti: (bi, ti, 0))
    tr = lambda w: pl.BlockSpec((1, w, tm), lambda bi, ti: (bi, 0, ti))
    chunked = lambda w: pl.BlockSpec((1, 1, w, tm), lambda bi, ti: (bi, ti, 0, 0))
    blocks = pl.BlockSpec((1, tm // BLOCK, 128, BLOCK), lambda bi, ti: (bi, ti, 0, 0))
    tile_row = pl.BlockSpec((1, 1, 1, LANE), lambda bi, ti: (bi, ti, 0, 0))
    out_specs = [tr(512), tok(128), blocks, tr(HB * HEAD_PAD), tok(HB * HEAD_PAD), chunked(HB * V_B),
                 tr(512), tok(256), chunked(128), tile_row, tile_row]
    out_shape = [
        jax.ShapeDtypeStruct((b, 512, s), BF16), jax.ShapeDtypeStruct((b, s, 128), BF16),
        jax.ShapeDtypeStruct((b, s // BLOCK, 128, BLOCK), BF16),
        jax.ShapeDtypeStruct((b, HB * HEAD_PAD, s), BF16), jax.ShapeDtypeStruct((b, s, HB * HEAD_PAD), BF16),
        jax.ShapeDtypeStruct((b, nt, HB * V_B, tm), BF16),
        jax.ShapeDtypeStruct((b, 512, s), BF16), jax.ShapeDtypeStruct((b, s, 256), BF16),
        jax.ShapeDtypeStruct((b, nt, 128, tm), BF16),
        jax.ShapeDtypeStruct((b, nt, 1, LANE), F32), jax.ShapeDtypeStruct((b, nt, 1, LANE), F32),
    ]
    return pl.pallas_call(
        _prep_kernel, grid=(b, nt), in_specs=in_specs, out_specs=out_specs, out_shape=out_shape,
        compiler_params=pltpu.CompilerParams(dimension_semantics=("parallel", "parallel"),
                                             vmem_limit_bytes=VMEM_LIMIT),
        name="prep",
    )(x, lw["ln1_g"], lw["w_att"], lw["q_norm_g"], lw["w_uq"], lw["kv_norm_g"], lw["w_uk"], lw["w_uv"],
      lw["gqc"], lw["gkc"], tabs["gmat"], tabs["ghb"], tabs["ghc"], tabs["oneb"], tabs["onec"],
      tabs["bqc"], tabs["bqs"], tabs["bkc"], tabs["bks"], tabs["cqc"], tabs["cqs"], tabs["ckc"], tabs["cks"])


def _window_kernel(sink_ref, qt_ref, k_ref, vt_ref, bias_ref, o_ref, ot_ref):
    nblk = vt_ref.shape[1]
    wblk = TK_WINDOW // BLOCK
    i = pl.program_id(1)
    cs = jnp.clip(i * (TQ_WINDOW // BLOCK) - 1, 0, nblk - wblk)
    start = pl.multiple_of(cs * BLOCK, BLOCK)
    kwin = k_ref[0, pl.ds(start, TK_WINDOW), :]
    grp = HA_Q // HA_KV
    ones_rows = (lax.broadcasted_iota(jnp.int32, (ACC_ROWS - HD_A, TK_WINDOW), 0) == 0).astype(BF16)
    kgs, vexts = [], []
    for g in range(HA_KV):
        kgs.append(kwin[:, g * HD_A:(g + 1) * HD_A])
        vwin = jnp.concatenate([vt_ref[0, cs + r, g * HD_A:(g + 1) * HD_A, :] for r in range(wblk)], axis=1)
        vexts.append(jnp.concatenate([vwin, ones_rows], axis=0))

    npair = HA_Q // 2

    def scores(c):
        qt = jnp.concatenate([qt_ref[0, h * HD_A:(h + 1) * HD_A, :] for h in (2 * c, 2 * c + 1)], axis=1)
        return jnp.dot(kgs[2 * c // grp], qt, preferred_element_type=F32)

    pending = [scores(c) for c in range(SCORE_LOOKAHEAD)]
    for c in range(npair):
        st = pending.pop(0)
        if c + SCORE_LOOKAHEAD < npair:
            pending.append(scores(c + SCORE_LOOKAHEAD))
        st = st + bias_ref[0, c]
        sink = sink_ref[c]
        m = jnp.maximum(jnp.max(st, axis=0, keepdims=True), sink)
        p = jnp.exp2(st - m).astype(BF16)
        pv = jnp.dot(vexts[2 * c // grp], p, preferred_element_type=F32)
        o2 = pv[0:HD_A, :] / (pv[HD_A:HD_A + 1, :] + jnp.exp2(sink - m))
        for r in range(2):
            h = 2 * c + r
            ot_ref[h * HD_A:(h + 1) * HD_A, :] = o2[:, r * TQ_WINDOW:(r + 1) * TQ_WINDOW]
    o_ref[0] = ot_ref[...].T.astype(BF16)


def _window_call(qat, ka, vat, sink, bias3):
    b, _, s = qat.shape
    nq = s // TQ_WINDOW
    variant = lambda bi, i: (jnp.where(i == 0, 0, jnp.where(i == nq - 1, 2, 1)), 0, 0, 0)
    return pl.pallas_call(
        _window_kernel, grid=(b, nq),
        in_specs=[
            pl.BlockSpec((HA_Q // 2, 1, 2 * TQ_WINDOW), lambda bi, i: (0, 0, 0)),
            pl.BlockSpec((1, 512, TQ_WINDOW), lambda bi, i: (bi, 0, i)),
            pl.BlockSpec((1, s, 128), lambda bi, i: (bi, 0, 0)),
            pl.BlockSpec((1, s // BLOCK, 128, BLOCK), lambda bi, i: (bi, 0, 0, 0)),
            pl.BlockSpec((1, HA_Q // 2, TK_WINDOW, 2 * TQ_WINDOW), variant),
        ],
        out_specs=pl.BlockSpec((1, TQ_WINDOW, 512), lambda bi, i: (bi, i, 0)),
        out_shape=jax.ShapeDtypeStruct((b, s, 512), BF16),
        scratch_shapes=[pltpu.VMEM((HA_Q * HD_A, TQ_WINDOW), F32)],
        compiler_params=pltpu.CompilerParams(dimension_semantics=("parallel", "arbitrary"),
                                             vmem_limit_bytes=VMEM_LIMIT),
        name="window_attn",
    )(sink, qat, ka, vat, bias3)


def _dense_kernel(qt_ref, k_ref, vt_ref, kn_ref, o_ref, acc_ref, ot_ref, st_ref, qs_ref, *, n_heads,
                  heads_per_kv, dq, dk, k_stride, ones_lane):
    nchunk = vt_ref.shape[1]
    tkc = vt_ref.shape[3]
    tq = qt_ref.shape[2]
    tk = CHUNKS_PER_TRIP * tkc
    ntrip = nchunk // CHUNKS_PER_TRIP
    ones_rows = (lax.broadcasted_iota(jnp.int32, (ACC_ROWS - 64, tkc), 0) == 0).astype(BF16)

    kn2 = jnp.max(kn_ref[0, :, 0, :], axis=0, keepdims=True)
    lane = lax.broadcasted_iota(jnp.int32, kn2.shape, 1)
    row = lax.broadcasted_iota(jnp.int32, (HEAD_PAD, tq), 0)
    bound = None
    for h in range(n_heads):
        g = h // heads_per_kv
        q = qt_ref[0, h * dq:h * dq + dk, :]
        if dk < HEAD_PAD:
            q = jnp.concatenate([q, jnp.zeros((HEAD_PAD - dk, tq), BF16)], axis=0)
        qf = q.astype(F32)
        kmax2 = jnp.sum(jnp.where(lane == g, kn2, 0.0), axis=1, keepdims=True)
        u = jnp.sqrt(jnp.sum(qf * qf, axis=0, keepdims=True) * kmax2) * BOUND_SLACK
        qs_ref[h * HEAD_PAD:(h + 1) * HEAD_PAD, :] = jnp.where(row == ones_lane, -u, qf).astype(BF16)
        hmax = jnp.max(u)
        bound = hmax if bound is None else jnp.maximum(bound, hmax)
    shifted_ok = bound <= SHIFT_BOUND_MAX

    def scores(t, h, shifted):
        g = h // heads_per_kv
        off = pl.multiple_of(t * tk, tk)
        if shifted:
            qt = qs_ref[h * HEAD_PAD:(h + 1) * HEAD_PAD, :]
            kc = k_ref[0, pl.ds(off, tk), g * k_stride:g * k_stride + HEAD_PAD]
        else:
            qt = qt_ref[0, h * dq:h * dq + dk, :]
            kc = k_ref[0, pl.ds(off, tk), g * k_stride:g * k_stride + dk]
        return jnp.dot(kc, qt, preferred_element_type=F32)

    def attend(shifted):
        acc_ref[...] = jnp.zeros(acc_ref.shape, F32)
        for j in range(SCORE_LOOKAHEAD):
            st_ref[j] = scores(0, j, shifted)

        def body(t, ms):
            t_next = jnp.minimum(t + 1, ntrip - 1)
            pending = [st_ref[j] for j in range(SCORE_LOOKAHEAD)]
            new_ms = []
            for h in range(n_heads):
                g = h // heads_per_kv
                st = pending.pop(0)
                ahead = h + SCORE_LOOKAHEAD
                if ahead < n_heads:
                    pending.append(scores(t, ahead, shifted))
                else:
                    st_ref[ahead - n_heads] = scores(t_next, ahead - n_heads, shifted)
                if shifted:
                    p = jnp.exp2(st).astype(BF16)
                else:
                    m_new = jnp.maximum(ms[h], jnp.max(st, axis=0, keepdims=True))
                    alpha = jnp.exp2(ms[h] - m_new)
                    p = jnp.exp2(st - m_new).astype(BF16)
                    new_ms.append(m_new)
                pv = None
                for cc in range(CHUNKS_PER_TRIP):
                    vc = vt_ref[0, t * CHUNKS_PER_TRIP + cc, g * 64:(g + 1) * 64, :]
                    part = jnp.dot(jnp.concatenate([vc, ones_rows], axis=0), p[cc * tkc:(cc + 1) * tkc, :],
                                   preferred_element_type=F32)
                    pv = part if pv is None else pv + part
                rows = slice(h * ACC_ROWS, (h + 1) * ACC_ROWS)
                if shifted:
                    acc_ref[rows, :] = acc_ref[rows, :] + pv
                else:
                    acc_ref[rows, :] = alpha * acc_ref[rows, :] + pv
            return tuple(new_ms)

        init = () if shifted else (jnp.full((1, tq), -jnp.inf, F32),) * n_heads
        lax.fori_loop(0, ntrip, body, init)

    pl.when(shifted_ok)(functools.partial(attend, True))
    pl.when(jnp.logical_not(shifted_ok))(functools.partial(attend, False))

    for h in range(n_heads):
        num = acc_ref[h * ACC_ROWS:h * ACC_ROWS + 64, :]
        den = acc_ref[h * ACC_ROWS + 64:h * ACC_ROWS + 65, :]
        ot_ref[h * 64:(h + 1) * 64, :] = num * (1.0 / den)
    o_ref[0] = ot_ref[...].T.astype(BF16)


def _dense_call(qt, k, vt, kn, *, n_heads, heads_per_kv, dq, dk, k_stride, ones_lane, name):
    b, fq, s = qt.shape
    tq = TQ_DENSE
    fk = k.shape[2]
    _, nchunk, fv, tk = vt.shape
    nt = kn.shape[1]
    kern = functools.partial(_dense_kernel, n_heads=n_heads, heads_per_kv=heads_per_kv, dq=dq, dk=dk,
                             k_stride=k_stride, ones_lane=ones_lane)
    return pl.pallas_call(
        kern, grid=(b, s // tq),
        in_specs=[
            pl.BlockSpec((1, fq, tq), lambda bi, i: (bi, 0, i)),
            pl.BlockSpec((1, s, fk), lambda bi, i: (bi, 0, 0)),
            pl.BlockSpec((1, nchunk, fv, tk), lambda bi, i: (bi, 0, 0, 0)),
            pl.BlockSpec((1, nt, 1, LANE), lambda bi, i: (bi, 0, 0, 0)),
        ],
        out_specs=pl.BlockSpec((1, tq, n_heads * 64), lambda bi, i: (bi, i, 0)),
        out_shape=jax.ShapeDtypeStruct((b, s, n_heads * 64), BF16),
        scratch_shapes=[pltpu.VMEM((n_heads * ACC_ROWS, tq), F32), pltpu.VMEM((n_heads * 64, tq), F32),
                        pltpu.VMEM((SCORE_LOOKAHEAD, CHUNKS_PER_TRIP * tk, tq), F32),
                        pltpu.VMEM((n_heads * HEAD_PAD, tq), BF16)],
        compiler_params=pltpu.CompilerParams(dimension_semantics=("parallel", "arbitrary"),
                                             vmem_limit_bytes=VMEM_LIMIT),
        name=name,
    )(qt, k, vt, kn)


def _merge_kernel(x_ref, oa_ref, ob_ref, oc_ref, g1_ref, wg_ref, wa_ref, wb_ref, wc_ref, wo_ref,
                  g2_ref, wr_ref, br_ref, x1_ref, h2_ref, lg_ref):
    x = x_ref[...]
    hb = _rms(x, g1_ref[...]).astype(BF16)
    merged = None
    for j, (o_ref, w_ref) in enumerate(((oa_ref, wa_ref), (ob_ref, wb_ref), (oc_ref, wc_ref))):
        gl = jnp.dot(hb, wg_ref[:, j * D_MODEL:(j + 1) * D_MODEL], preferred_element_type=F32)
        y = jnp.dot(o_ref[...], w_ref[...], preferred_element_type=F32)
        t = jax.nn.sigmoid(gl) * y
        merged = t if merged is None else merged + t
    x1 = x + jnp.dot(merged.astype(BF16), wo_ref[...], preferred_element_type=F32)
    x1_ref[...] = x1
    h2 = _rms(x1, g2_ref[...])
    h2b = h2.astype(BF16)
    h2_ref[...] = h2b
    lg_ref[...] = jnp.dot(h2b, wr_ref[...], preferred_element_type=F32) + br_ref[...]


def _merge_call(x2d, oa, ob, oc, lw):
    t, d = x2d.shape
    tm = TM_MERGE
    const = lambda *shape: pl.BlockSpec(shape, lambda i: (0,) * len(shape))
    tok = lambda w: pl.BlockSpec((tm, w), lambda i: (i, 0))
    return pl.pallas_call(
        _merge_kernel, grid=(t // tm,),
        in_specs=[tok(d), tok(512), tok(512), tok(512), const(1, d), const(d, N_BRANCHES * d),
                  const(512, d), const(512, d), const(512, d), const(d, d), const(1, d),
                  const(d, ROUTER_COLS), const(1, ROUTER_COLS)],
        out_specs=[tok(d), tok(d), tok(ROUTER_COLS)],
        out_shape=[jax.ShapeDtypeStruct((t, d), F32), jax.ShapeDtypeStruct((t, d), BF16),
                   jax.ShapeDtypeStruct((t, ROUTER_COLS), F32)],
        compiler_params=pltpu.CompilerParams(dimension_semantics=("parallel",),
                                             vmem_limit_bytes=VMEM_LIMIT),
        name="merge",
    )(x2d, oa, ob, oc, lw["ln1_g"], lw["w_gate3"], lw["w_br_a"], lw["w_br_b"], lw["w_br_c"], lw["w_out"],
      lw["ln2_g"], lw["w_router"], lw["b_router"])


def _combine_weights(lg):
    lane = lax.broadcasted_iota(jnp.int32, lg.shape, 1)
    big = jnp.int32(1 << 20)
    ninf = jnp.float32(-jnp.inf)
    is_g = lane < N_GROUPS
    gl = jnp.where(is_g, lg, ninf)
    gmax = jnp.max(gl, axis=-1, keepdims=True)
    grp = jnp.min(jnp.where(is_g & (lg == gmax), lane, big), axis=-1, keepdims=True)
    gsum = jnp.sum(jnp.where(is_g, jnp.exp(gl - gmax), 0.0), axis=-1, keepdims=True)
    grp_w = 1.0 / gsum
    lo = EXPERT_LANE0 + grp * EXP_PER_GROUP
    in_grp = (lane >= lo) & (lane < lo + EXP_PER_GROUP)
    v1 = jnp.max(jnp.where(in_grp, lg, ninf), axis=-1, keepdims=True)
    i1 = jnp.min(jnp.where(in_grp & (lg == v1), lane, big), axis=-1, keepdims=True)
    rest = in_grp & (lane != i1)
    v2 = jnp.max(jnp.where(rest, lg, ninf), axis=-1, keepdims=True)
    i2 = jnp.min(jnp.where(rest & (lg == v2), lane, big), axis=-1, keepdims=True)
    e21 = jnp.exp(v2 - v1)
    den = 1.0 + e21
    w1 = (1.0 / den) * grp_w
    w2 = (e21 / den) * grp_w
    return jnp.where(lane == i1, w1, 0.0) + jnp.where(lane == i2, w2, 0.0), grp


def _moe_kernel(x1_ref, h2_ref, lg_ref, ltri_ref, wgu_ref, wd_ref, gf_ref, o_ref, xs_ref, cs_ref, ys_ref, *,
                final_norm):
    tm = h2_ref.shape[0]
    nblk = tm // MOE_BLOCK
    comb, grp = _combine_weights(lg_ref[...])
    lane = lax.broadcasted_iota(jnp.int32, comb.shape, 1)
    lane1 = lax.broadcasted_iota(jnp.int32, (1, ROUTER_COLS), 1)
    onehot = lane == grp
    cnt = jnp.sum(jnp.where(onehot, 1.0, 0.0), axis=0, keepdims=True)
    offs = [jnp.sum(jnp.where(lane1 < g, cnt, 0.0)) for g in range(N_GROUPS)]
    off_row = sum(jnp.where(lane1 == g, offs[g], 0.0) for g in range(N_GROUPS))
    rank = jnp.dot(ltri_ref[...], jnp.where(onehot, 1.0, 0.0).astype(BF16), preferred_element_type=F32)
    pos = jnp.sum(jnp.where(onehot, rank + off_row, 0.0), axis=1, keepdims=True).astype(jnp.int32)
    pos_row = jnp.where(lane == 0, pos, 0).astype(F32).T[0:1, :].astype(jnp.int32)
    perm = jnp.where(lax.broadcasted_iota(jnp.int32, (tm, tm), 0) == pos_row, 1.0, 0.0).astype(BF16)
    perm_t = jnp.where(lax.broadcasted_iota(jnp.int32, (tm, tm), 1) == pos, 1.0, 0.0).astype(BF16)

    c_hi = comb.astype(BF16)
    c_lo = (comb - c_hi.astype(F32)).astype(BF16)
    srt = jnp.dot(perm, jnp.concatenate([h2_ref[...], c_hi, c_lo], axis=1), preferred_element_type=F32)
    d = h2_ref.shape[1]
    xs_ref[...] = srt[:, 0:d].astype(BF16)
    cs_ref[...] = srt[:, d:d + ROUTER_COLS] + srt[:, d + ROUTER_COLS:d + 2 * ROUTER_COLS]
    ys_ref[...] = jnp.zeros(ys_ref.shape, F32)

    starts = [o.astype(jnp.int32) for o in offs]
    passes = []
    for b in range(nblk):
        g_first = sum((starts[g] <= b * MOE_BLOCK).astype(jnp.int32) for g in range(1, N_GROUPS))
        passes.append((jnp.int32(b), g_first, jnp.float32(1.0)))
    for g in range(1, N_GROUPS):
        inside = jnp.logical_and(starts[g] % MOE_BLOCK != 0, starts[g] < tm)
        passes.append((jnp.minimum(starts[g] // MOE_BLOCK, nblk - 1), jnp.int32(g), inside.astype(F32)))

    def gate_up(ps):
        b, g, _ = ps
        xb = xs_ref[pl.ds(pl.multiple_of(b * MOE_BLOCK, MOE_BLOCK), MOE_BLOCK), :]
        return [jnp.dot(xb, wgu_ref[g * EXP_PER_GROUP + j], preferred_element_type=F32)
                for j in range(EXP_PER_GROUP)]

    def finish(ps, gus):
        b, g, live = ps
        rows = pl.ds(pl.multiple_of(b * MOE_BLOCK, MOE_BLOCK), MOE_BLOCK)
        cb = cs_ref[rows, :] * live
        lane_b = lax.broadcasted_iota(jnp.int32, cb.shape, 1)
        acts = []
        for j in range(EXP_PER_GROUP):
            ce = jnp.sum(jnp.where(lane_b == EXPERT_LANE0 + g * EXP_PER_GROUP + j, cb, 0.0), axis=1,
                         keepdims=True)
            gt = gus[j][:, 0:D_EXPERT]
            acts.append((gt * jax.nn.sigmoid(gt) * gus[j][:, D_EXPERT:2 * D_EXPERT] * ce).astype(BF16))
        wrows = pl.ds(pl.multiple_of(g * (EXP_PER_GROUP * D_EXPERT), EXP_PER_GROUP * D_EXPERT),
                      EXP_PER_GROUP * D_EXPERT)
        y = jnp.dot(jnp.concatenate(acts, axis=1), wd_ref[wrows, :], preferred_element_type=F32)
        ys_ref[rows, :] = ys_ref[rows, :] + y

    pending = [gate_up(passes[0])]
    for i, ps in enumerate(passes):
        gus = pending.pop(0)
        if i + 1 < len(passes):
            pending.append(gate_up(passes[i + 1]))
        finish(ps, gus)

    ys = ys_ref[...]
    y_hi = ys.astype(BF16)
    y_lo = (ys - y_hi.astype(F32)).astype(BF16)
    y = x1_ref[...] + (jnp.dot(perm_t, y_hi, preferred_element_type=F32)
                       + jnp.dot(perm_t, y_lo, preferred_element_type=F32))
    if final_norm:
        y = _rms(y, gf_ref[...])
    o_ref[...] = y


def _moe_call(x1, h2, lg, lw, final_g, final_norm):
    t, d = x1.shape
    tm = TM_MOE
    tok = lambda w: pl.BlockSpec((tm, w), lambda i: (i, 0))
    once = pl.Buffered(1)
    idx = jnp.arange(tm)
    ltri = (idx[None, :] < idx[:, None]).astype(BF16)
    return pl.pallas_call(
        functools.partial(_moe_kernel, final_norm=final_norm), grid=(t // tm,),
        in_specs=[tok(d), tok(d), tok(ROUTER_COLS),
                  pl.BlockSpec((tm, tm), lambda i: (0, 0), pipeline_mode=once),
                  pl.BlockSpec((N_EXPERTS, d, 2 * D_EXPERT), lambda i: (0, 0, 0), pipeline_mode=once),
                  pl.BlockSpec((N_EXPERTS * D_EXPERT, d), lambda i: (0, 0), pipeline_mode=once),
                  pl.BlockSpec((1, d), lambda i: (0, 0))],
        out_specs=tok(d),
        out_shape=jax.ShapeDtypeStruct((t, d), F32),
        scratch_shapes=[pltpu.VMEM((tm, d), BF16), pltpu.VMEM((tm, ROUTER_COLS), F32),
                        pltpu.VMEM((tm, d), F32)],
        compiler_params=pltpu.CompilerParams(dimension_semantics=("parallel",),
                                             vmem_limit_bytes=VMEM_LIMIT),
        name="moe",
    )(x1, h2, lg, ltri, lw["w_gu"], lw["w_dn"], final_g)


def _layer_weights(l, w_in, ln1_g, sink_a, q_norm_g, w_uq, kv_norm_g, w_ukv, q_norm_c, k_norm_c,
                   w_br_a, w_br_b, w_br_c, w_out, ln2_g, w_grp, b_grp, w_exr, b_exr, w_gate, w_up, w_down):
    wi = w_in[l]
    d = wi.shape[0]
    z = lambda n: jnp.zeros((d, n), F32)
    w_att = jnp.concatenate([
        wi[:, 0:1152], z(64), wi[:, 1152:1184], z(32), wi[:, 1184:1696],
        wi[:, 1696:1760], z(64), wi[:, 1760:1824], z(64), wi[:, 1824:1952]], axis=1).astype(BF16)
    wq = w_uq[l].reshape(Q_LORA, HB, NOPE_B + ROPE_B)
    wq = jnp.concatenate([wq, jnp.zeros((Q_LORA, HB, HEAD_PAD - NOPE_B - ROPE_B), F32)], axis=-1)
    wkv = w_ukv[l].reshape(KV_LORA, HB, NOPE_B + V_B)
    wk = jnp.concatenate([wkv[..., :NOPE_B], jnp.zeros((KV_LORA, HB, HEAD_PAD - NOPE_B), F32)], axis=-1)
    zero64 = jnp.zeros((HD_C,), F32)
    w_router = jnp.concatenate([w_grp[l], w_exr[l], jnp.zeros((d, ROUTER_COLS - N_GROUPS - N_EXPERTS), F32)], 1)
    b_router = jnp.concatenate([b_grp[l], b_exr[l], jnp.zeros((ROUTER_COLS - N_GROUPS - N_EXPERTS,), F32)])
    return dict(
        ln1_g=ln1_g[l].reshape(1, d), w_att=w_att, w_gate3=wi[:, 1952:].astype(BF16),
        sink=jnp.repeat(sink_a[l] * LOG2E, TQ_WINDOW).reshape(HA_Q // 2, 1, 2 * TQ_WINDOW), q_norm_g=q_norm_g[l].reshape(1, Q_LORA),
        w_uq=wq.reshape(Q_LORA, HB * HEAD_PAD).astype(BF16),
        kv_norm_g=kv_norm_g[l].reshape(1, KV_LORA),
        w_uk=wk.reshape(KV_LORA, HB * HEAD_PAD).astype(BF16),
        w_uv=wkv[..., NOPE_B:].reshape(KV_LORA, HB * V_B).astype(BF16),
        gqc=jnp.tile(q_norm_c[l], HC_Q).reshape(1, 512),
        gkc=jnp.concatenate([k_norm_c[l], zero64, k_norm_c[l], zero64]).reshape(1, 256),
        w_br_a=w_br_a[l].astype(BF16), w_br_b=w_br_b[l].astype(BF16), w_br_c=w_br_c[l].astype(BF16),
        w_out=w_out[l].astype(BF16), ln2_g=ln2_g[l].reshape(1, d),
        w_router=w_router.astype(BF16), b_router=b_router.reshape(1, ROUTER_COLS),
        w_gu=jnp.concatenate([w_gate[l], w_up[l]], axis=-1).astype(BF16),
        w_dn=w_down[l].reshape(N_EXPERTS * D_EXPERT, d).astype(BF16),
    )


def _t5_bucket(rel):
    nb = N_BUCKETS // 2
    bucket = jnp.where(rel > 0, nb, 0)
    n = jnp.abs(rel)
    max_exact = nb // 2
    nf = jnp.maximum(n, 1).astype(F32)
    large = max_exact + (jnp.log(nf / max_exact) / math.log(MAX_DISTANCE / max_exact)
                         * (nb - max_exact)).astype(jnp.int32)
    large = jnp.minimum(large, nb - 1)
    return bucket + jnp.where(n < max_exact, n, large)


def _tables(s, rel_bias):
    half = ROPE_B // 2
    inv = ROPE_THETA ** (-jnp.arange(half, dtype=F32) / half)
    pos = jnp.arange(s, dtype=jnp.int32)
    row = (pos // GRID_W).astype(F32)
    col = (pos % GRID_W).astype(F32)

    def cs(p):
        ang = p[:, None] * inv[None, :]
        return jnp.cos(ang), jnp.sin(ang)

    cp, sp = cs(pos.astype(F32))
    cr, sr = cs(row)
    cc, sc = cs(col)
    one = jnp.ones((s, 64), F32)
    z32 = jnp.zeros((s, 32), F32)
    z64 = jnp.zeros((s, 64), F32)
    b_cos = jnp.concatenate([one, cp, cp, z32], axis=1)
    b_sin = jnp.concatenate([z64, -sp, sp, z32], axis=1)
    c_cos = jnp.concatenate([cr, cr, cc, cc], axis=1)
    c_sin = jnp.concatenate([-sr, sr, -sc, sc], axis=1)
    scale_b = (NOPE_B + ROPE_B) ** -0.5 * LOG2E
    scale_c = HD_C ** -0.5 * LOG2E
    gidx = jnp.arange(512) // HD_C
    gmat = (gidx[:, None] == gidx[None, :]).astype(BF16)
    period = 4 * TQ_WINDOW + 1
    mats = []
    for shift in (0, WINDOW, 2 * WINDOW):
        rel = jnp.arange(period) - TQ_WINDOW - shift
        prof = jnp.where((jnp.abs(rel) <= WINDOW)[:, None], rel_bias[_t5_bucket(rel)].astype(F32) * LOG2E,
                         NEG_INF).T
        skew = jnp.tile(prof, (1, TQ_WINDOW))[:, :TQ_WINDOW * (period - 1)]
        skew = skew.reshape(HA_Q, TQ_WINDOW, period - 1)[:, :, TQ_WINDOW:TQ_WINDOW + TK_WINDOW]
        bias = jnp.transpose(skew, (0, 2, 1)).reshape(HA_Q // 2, 2, TK_WINDOW, TQ_WINDOW)
        mats.append(jnp.transpose(bias, (0, 2, 1, 3)).reshape(HA_Q // 2, TK_WINDOW, 2 * TQ_WINDOW))
    return dict(
        bqc=b_cos * scale_b, bqs=b_sin * scale_b, bkc=b_cos, bks=b_sin,
        cqc=jnp.concatenate([c_cos, c_cos], 1) * scale_c, cqs=jnp.concatenate([c_sin, c_sin], 1) * scale_c,
        ckc=jnp.concatenate([c_cos, z64], 1), cks=jnp.concatenate([c_sin, z64], 1),
        gmat=gmat, bias3=jnp.stack(mats, axis=0), **_key_norm_constants())


def _key_norm_constants():
    lb = jnp.arange(HB * HEAD_PAD)
    ghb = ((lb[:, None] // HEAD_PAD == jnp.arange(LANE)[None, :]) & (lb[:, None] % HEAD_PAD < ONES_LANE_B))
    lc = jnp.arange(HC_KV * LANE)
    ghc = ((lc[:, None] // LANE == jnp.arange(LANE)[None, :]) & (lc[:, None] % LANE < ONES_LANE_C))
    return dict(ghb=ghb.astype(BF16), ghc=ghc.astype(BF16),
                oneb=(lb % HEAD_PAD == ONES_LANE_B).astype(F32).reshape(1, -1),
                onec=(lc % LANE == ONES_LANE_C).astype(F32).reshape(1, -1))


def _trunk(x, layers, tabs, final_g):
    b, s, d = x.shape
    n_layers = len(layers)
    for l, lw in enumerate(layers):
        qa, ka, va, qbt, kb, vbt, qct, kc, vct, kbn, kcn = _prep_call(x, lw, tabs)
        oa = _window_call(qa, ka, va, lw["sink"], tabs["bias3"])
        ob = _dense_call(qbt, kb, vbt, kbn, n_heads=HB, heads_per_kv=1, dq=HEAD_PAD, dk=HEAD_PAD,
                         k_stride=HEAD_PAD, ones_lane=ONES_LANE_B, name="latent_attn")
        oc = _dense_call(qct, kc, vct, kcn, n_heads=HC_Q, heads_per_kv=HC_Q // HC_KV, dq=HD_C, dk=HD_C,
                         k_stride=LANE, ones_lane=ONES_LANE_C, name="axial_attn")
        t = b * s
        x1, h2, lg = _merge_call(x.reshape(t, d), oa.reshape(t, 512), ob.reshape(t, 512),
                                 oc.reshape(t, 512), lw)
        x = _moe_call(x1, h2, lg, lw, final_g, l == n_layers - 1).reshape(b, s, d)
    return x


def kernel(x_prompt, x_sample, rel_bias, final_g, ln1_g, w_in, sink_a, q_norm_g, w_uq, kv_norm_g, w_ukv,
           q_norm_c, k_norm_c, w_br_a, w_br_b, w_br_c, w_out, ln2_g, w_grp, b_grp, w_exr, b_exr, w_gate,
           w_up, w_down):
    n_layers = w_in.shape[0]
    layers = [_layer_weights(l, w_in, ln1_g, sink_a, q_norm_g, w_uq, kv_norm_g, w_ukv, q_norm_c, k_norm_c,
                             w_br_a, w_br_b, w_br_c, w_out, ln2_g, w_grp, b_grp, w_exr, b_exr, w_gate,
                             w_up, w_down) for l in range(n_layers)]
    fg = final_g.reshape(1, -1)
    outs = []
    for x in (x_prompt, x_sample):
        tabs = _tables(x.shape[1], rel_bias)
        outs.append(_trunk(x, layers, tabs, fg))
    return tuple(outs)
```

```python
import functools
import math

import jax
import jax.numpy as jnp
from jax import lax
from jax.experimental import pallas as pl
from jax.experimental.pallas import tpu as pltpu

F32 = jnp.float32
BF16 = jnp.bfloat16

D_MODEL = 1024
GRID_W = 64
BLOCK = 128
EPS = 1e-6
NEG_INF = -1e30
ROPE_THETA = 10000.0
HA_Q, HA_KV, HD_A = 8, 2, 64
WINDOW = 128
N_BUCKETS = 32
MAX_DISTANCE = 128
HB, Q_LORA, KV_LORA, NOPE_B, ROPE_B, V_B = 8, 256, 128, 64, 32, 64
HC_Q, HC_KV, HD_C = 8, 2, 64
N_BRANCHES = 3
N_GROUPS, EXP_PER_GROUP = 4, 4
N_EXPERTS = N_GROUPS * EXP_PER_GROUP
D_EXPERT = 256

LANE = 128
HEAD_PAD = 128
ATT_COLS = 2176
ROUTER_COLS = 128
GROUP_LANE0 = 0
EXPERT_LANE0 = N_GROUPS

TM_PREP = 256
TK_DENSE = TM_PREP
TQ_DENSE = 1024
TQ_WINDOW = 256
TK_WINDOW = TQ_WINDOW + 2 * WINDOW
WINDOW_SUBTILES = 2
CHUNKS_PER_TRIP = 1
SCORE_LOOKAHEAD = 2
ACC_ROWS = 80
LOG2E = math.log2(math.e)
ONES_LANE_B = NOPE_B + ROPE_B
ONES_LANE_C = HD_C
BOUND_SLACK = 1.02
SHIFT_BOUND_MAX = 48.0
TM_MERGE = 256
TM_MOE = 512
MOE_BLOCK = 128
VMEM_LIMIT = 56 * 1024 * 1024


def _rms(x, g):
    return x * lax.rsqrt(jnp.mean(x * x, axis=-1, keepdims=True) + EPS) * g


def _swap16(x):
    n = x.shape[-1]
    lane = lax.broadcasted_iota(jnp.int32, x.shape, x.ndim - 1)
    return jnp.where((lane & 16) == 0, pltpu.roll(x, n - 16, x.ndim - 1), pltpu.roll(x, 16, x.ndim - 1))


def _tile_lanes(t, reps):
    return jnp.concatenate([t] * reps, axis=-1) if reps > 1 else t


def _group_ssq(x, g_ref):
    sq = x * x
    hi = sq.astype(BF16)
    lo = (sq - hi.astype(F32)).astype(BF16)
    g = g_ref[...]
    return (jnp.dot(hi, g, preferred_element_type=F32) + jnp.dot(lo, g, preferred_element_type=F32))


def _max_sq_norm(k, g_ref):
    kr = k.astype(BF16).astype(F32)
    sq = jnp.dot((kr * kr).astype(BF16), g_ref[...], preferred_element_type=F32)
    return jnp.max(sq, axis=0, keepdims=True)


def _prep_kernel(x_ref, g1_ref, w_ref, qng_ref, wuq_ref, kvng_ref, wuk_ref, wuv_ref,
                 gqc_ref, gkc_ref, gmat_ref, ghb_ref, ghc_ref, oneb_ref, onec_ref,
                 bqc_ref, bqs_ref, bkc_ref, bks_ref, cqc_ref, cqs_ref, ckc_ref, cks_ref,
                 qa_ref, ka_ref, va_ref, qbt_ref, kb_ref, vbt_ref, qct_ref, kc_ref, vct_ref,
                 kbn_ref, kcn_ref):
    x = x_ref[0]
    hb = _rms(x, g1_ref[...]).astype(BF16)

    pb = jnp.dot(hb, w_ref[:, 768:1280], preferred_element_type=F32)
    pc = jnp.dot(hb, w_ref[:, 1280:2176], preferred_element_type=F32)
    pa = jnp.dot(hb, w_ref[:, 0:768], preferred_element_type=F32)
    cq = _rms(pb[:, 0:256], qng_ref[...]).astype(BF16)
    ckv = _rms(pb[:, 256:384], kvng_ref[...]).astype(BF16)
    qb = jnp.dot(cq, wuq_ref[...], preferred_element_type=F32)
    kb = jnp.dot(ckv, wuk_ref[...], preferred_element_type=F32)
    vb = jnp.dot(ckv, wuv_ref[...], preferred_element_type=F32)
    qc = pc[:, 0:512]
    kc = pc[:, 512:768]
    ssq_q = _group_ssq(qc, gmat_ref)
    ssq_k = _group_ssq(kc, gmat_ref.at[0:256, 0:256])

    qa_ref[0] = (pa[:, 0:512] * (HD_A ** -0.5 * LOG2E)).T.astype(BF16)
    ka_ref[0] = pa[:, 512:640].astype(BF16)
    vat = pa[:, 640:768].T.astype(BF16)
    for r in range(TM_PREP // BLOCK):
        va_ref[0, r] = vat[:, r * BLOCK:(r + 1) * BLOCK]

    qb = qb * _tile_lanes(bqc_ref[...], HB) + _swap16(qb) * _tile_lanes(bqs_ref[...], HB)
    qbt_ref[0] = qb.T.astype(BF16)
    kpe = pb[:, 384:512]
    kpe = kpe * bkc_ref[...] + _swap16(kpe) * bks_ref[...]
    kb = kb + _tile_lanes(kpe, HB)
    kbn_ref[0, 0] = _max_sq_norm(kb, ghb_ref)
    kb_ref[0] = (kb + oneb_ref[...]).astype(BF16)
    vbt_ref[0, 0] = vb.T.astype(BF16)

    qc = qc * lax.rsqrt(ssq_q * (1.0 / HD_C) + EPS) * gqc_ref[...]
    qc = qc * _tile_lanes(cqc_ref[...], 4) + _swap16(qc) * _tile_lanes(cqs_ref[...], 4)
    qct_ref[0] = qc.T.astype(BF16)
    kc = kc * lax.rsqrt(ssq_k * (1.0 / HD_C) + EPS) * gkc_ref[...]
    kc = kc * _tile_lanes(ckc_ref[...], 2) + _swap16(kc) * _tile_lanes(cks_ref[...], 2)
    kcn_ref[0, 0] = _max_sq_norm(kc, ghc_ref)
    kc_ref[0] = (kc + onec_ref[...]).astype(BF16)
    vct_ref[0, 0] = pc[:, 768:896].T.astype(BF16)


def _prep_call(x, lw, tabs):
    b, s, d = x.shape
    tm = TM_PREP
    nt = s // tm
    const = lambda *shape: pl.BlockSpec(shape, lambda bi, ti: (0,) * len(shape))
    tab = lambda w: pl.BlockSpec((tm, w), lambda bi, ti: (ti, 0))
    in_specs = [
        pl.BlockSpec((1, tm, d), lambda bi, ti: (bi, ti, 0)),
        const(1, d), const(d, ATT_COLS), const(1, Q_LORA), const(Q_LORA, HB * HEAD_PAD),
        const(1, KV_LORA), const(KV_LORA, HB * HEAD_PAD), const(KV_LORA, HB * V_B),
        const(1, 512), const(1, 256), const(512, 512),
        const(HB * HEAD_PAD, LANE), const(256, LANE), const(1, HB * HEAD_PAD), const(1, 256),
        tab(LANE), tab(LANE), tab(LANE), tab(LANE), tab(LANE), tab(LANE), tab(LANE), tab(LANE),
    ]
    tok = lambda w: pl.BlockSpec((1, tm, w), lambda bi, ti: (bi, ti, 0))
    tr = lambda w: pl.BlockSpec((1, w, tm), lambda bi, ti: (bi, 0, ti))
    chunked = lambda w: pl.BlockSpec((1, 1, w, tm), lambda bi, ti: (bi, ti, 0, 0))
    blocks = pl.BlockSpec((1, tm // BLOCK, 128, BLOCK), lambda bi, ti: (bi, ti, 0, 0))
    tile_row = pl.BlockSpec((1, 1, 1, LANE), lambda bi, ti: (bi, ti, 0, 0))
    out_specs = [tr(512), tok(128), blocks, tr(HB * HEAD_PAD), tok(HB * HEAD_PAD), chunked(HB * V_B),
                 tr(512), tok(256), chunked(128), tile_row, tile_row]
    out_shape = [
        jax.ShapeDtypeStruct((b, 512, s), BF16), jax.ShapeDtypeStruct((b, s, 128), BF16),
        jax.ShapeDtypeStruct((b, s // BLOCK, 128, BLOCK), BF16),
        jax.ShapeDtypeStruct((b, HB * HEAD_PAD, s), BF16), jax.ShapeDtypeStruct((b, s, HB * HEAD_PAD), BF16),
        jax.ShapeDtypeStruct((b, nt, HB * V_B, tm), BF16),
        jax.ShapeDtypeStruct((b, 512, s), BF16), jax.ShapeDtypeStruct((b, s, 256), BF16),
        jax.ShapeDtypeStruct((b, nt, 128, tm), BF16),
        jax.ShapeDtypeStruct((b, nt, 1, LANE), F32), jax.ShapeDtypeStruct((b, nt, 1, LANE), F32),
    ]
    return pl.pallas_call(
        _prep_kernel, grid=(b, nt), in_specs=in_specs, out_specs=out_specs, out_shape=out_shape,
        compiler_params=pltpu.CompilerParams(dimension_semantics=("parallel", "parallel"),
                                             vmem_limit_bytes=VMEM_LIMIT),
        name="prep",
    )(x, lw["ln1_g"], lw["w_att"], lw["q_norm_g"], lw["w_uq"], lw["kv_norm_g"], lw["w_uk"], lw["w_uv"],
      lw["gqc"], lw["gkc"], tabs["gmat"], tabs["ghb"], tabs["ghc"], tabs["oneb"], tabs["onec"],
      tabs["bqc"], tabs["bqs"], tabs["bkc"], tabs["bks"], tabs["cqc"], tabs["cqs"], tabs["ckc"], tabs["cks"])


def _window_kernel(sink_ref, qt_ref, k_ref, vt_ref, bias0_ref, bias1_ref, o_ref, ot_ref):
    nblk = vt_ref.shape[1]
    wblk = TK_WINDOW // BLOCK
    grp = HA_Q // HA_KV
    npair = HA_Q // 2
    bias_refs = (bias0_ref, bias1_ref)
    ones_rows = (lax.broadcasted_iota(jnp.int32, (ACC_ROWS - HD_A, TK_WINDOW), 0) == 0).astype(BF16)

    kgs, vexts = [], []
    for sub in range(WINDOW_SUBTILES):
        i = pl.program_id(1) * WINDOW_SUBTILES + sub
        cs = jnp.clip(i * (TQ_WINDOW // BLOCK) - 1, 0, nblk - wblk)
        kwin = k_ref[0, pl.ds(pl.multiple_of(cs * BLOCK, BLOCK), TK_WINDOW), :]
        for g in range(HA_KV):
            kgs.append(kwin[:, g * HD_A:(g + 1) * HD_A])
            vwin = jnp.concatenate([vt_ref[0, cs + r, g * HD_A:(g + 1) * HD_A, :] for r in range(wblk)],
                                   axis=1)
            vexts.append(jnp.concatenate([vwin, ones_rows], axis=0))

    def scores(n):
        sub, c = divmod(n, npair)
        cols = slice(sub * TQ_WINDOW, (sub + 1) * TQ_WINDOW)
        qt = jnp.concatenate([qt_ref[0, h * HD_A:(h + 1) * HD_A, cols] for h in (2 * c, 2 * c + 1)], axis=1)
        return jnp.dot(kgs[sub * HA_KV + 2 * c // grp], qt, preferred_element_type=F32)

    nchain = WINDOW_SUBTILES * npair
    pending = [scores(n) for n in range(SCORE_LOOKAHEAD)]
    for n in range(nchain):
        sub, c = divmod(n, npair)
        st = pending.pop(0)
        if n + SCORE_LOOKAHEAD < nchain:
            pending.append(scores(n + SCORE_LOOKAHEAD))
        st = st + bias_refs[sub][0, c]
        sink = sink_ref[c]
        m = jnp.maximum(jnp.max(st, axis=0, keepdims=True), sink)
        p = jnp.exp2(st - m).astype(BF16)
        pv = jnp.dot(vexts[sub * HA_KV + 2 * c // grp], p, preferred_element_type=F32)
        o2 = pv[0:HD_A, :] / (pv[HD_A:HD_A + 1, :] + jnp.exp2(sink - m))
        for r in range(2):
            h = 2 * c + r
            ot_ref[h * HD_A:(h + 1) * HD_A, sub * TQ_WINDOW:(sub + 1) * TQ_WINDOW] = (
                o2[:, r * TQ_WINDOW:(r + 1) * TQ_WINDOW])
    o_ref[0] = ot_ref[...].T.astype(BF16)


def _window_call(qat, ka, vat, sink, bias3):
    b, _, s = qat.shape
    nq = s // TQ_WINDOW
    tstep = WINDOW_SUBTILES * TQ_WINDOW

    def variant(sub):
        def index(bi, i):
            t = i * WINDOW_SUBTILES + sub
            return (jnp.where(t == 0, 0, jnp.where(t == nq - 1, 2, 1)), 0, 0, 0)
        return pl.BlockSpec((1, HA_Q // 2, TK_WINDOW, 2 * TQ_WINDOW), index)

    return pl.pallas_call(
        _window_kernel, grid=(b, s // tstep),
        in_specs=[
            pl.BlockSpec((HA_Q // 2, 1, 2 * TQ_WINDOW), lambda bi, i: (0, 0, 0)),
            pl.BlockSpec((1, 512, tstep), lambda bi, i: (bi, 0, i)),
            pl.BlockSpec((1, s, 128), lambda bi, i: (bi, 0, 0)),
            pl.BlockSpec((1, s // BLOCK, 128, BLOCK), lambda bi, i: (bi, 0, 0, 0)),
            variant(0), variant(1),
        ],
        out_specs=pl.BlockSpec((1, tstep, 512), lambda bi, i: (bi, i, 0)),
        out_shape=jax.ShapeDtypeStruct((b, s, 512), BF16),
        scratch_shapes=[pltpu.VMEM((HA_Q * HD_A, tstep), F32)],
        compiler_params=pltpu.CompilerParams(dimension_semantics=("parallel", "arbitrary"),
                                             vmem_limit_bytes=VMEM_LIMIT),
        name="window_attn",
    )(sink, qat, ka, vat, bias3, bias3)


def _dense_kernel(qt_ref, k_ref, vt_ref, kn_ref, o_ref, acc_ref, ot_ref, st_ref, qs_ref, *, n_heads,
                  heads_per_kv, dq, dk, k_stride, ones_lane):
    nchunk = vt_ref.shape[1]
    tkc = vt_ref.shape[3]
    tq = qt_ref.shape[2]
    tk = CHUNKS_PER_TRIP * tkc
    ntrip = nchunk // CHUNKS_PER_TRIP
    ones_rows = (lax.broadcasted_iota(jnp.int32, (ACC_ROWS - 64, tkc), 0) == 0).astype(BF16)

    kn2 = jnp.max(kn_ref[0, :, 0, :], axis=0, keepdims=True)
    lane = lax.broadcasted_iota(jnp.int32, kn2.shape, 1)
    row = lax.broadcasted_iota(jnp.int32, (HEAD_PAD, tq), 0)
    bound = None
    for h in range(n_heads):
        g = h // heads_per_kv
        q = qt_ref[0, h * dq:h * dq + dk, :]
        if dk < HEAD_PAD:
            q = jnp.concatenate([q, jnp.zeros((HEAD_PAD - dk, tq), BF16)], axis=0)
        qf = q.astype(F32)
        kmax2 = jnp.sum(jnp.where(lane == g, kn2, 0.0), axis=1, keepdims=True)
        u = jnp.sqrt(jnp.sum(qf * qf, axis=0, keepdims=True) * kmax2) * BOUND_SLACK
        qs_ref[h * HEAD_PAD:(h + 1) * HEAD_PAD, :] = jnp.where(row == ones_lane, -u, qf).astype(BF16)
        hmax = jnp.max(u)
        bound = hmax if bound is None else jnp.maximum(bound, hmax)
    shifted_ok = bound <= SHIFT_BOUND_MAX

    def scores(t, h, shifted):
        g = h // heads_per_kv
        off = pl.multiple_of(t * tk, tk)
        if shifted:
            qt = qs_ref[h * HEAD_PAD:(h + 1) * HEAD_PAD, :]
            kc = k_ref[0, pl.ds(off, tk), g * k_stride:g * k_stride + HEAD_PAD]
        else:
            qt = qt_ref[0, h * dq:h * dq + dk, :]
            kc = k_ref[0, pl.ds(off, tk), g * k_stride:g * k_stride + dk]
        return jnp.dot(kc, qt, preferred_element_type=F32)

    def attend(shifted):
        acc_ref[...] = jnp.zeros(acc_ref.shape, F32)
        for j in range(SCORE_LOOKAHEAD):
            st_ref[j] = scores(0, j, shifted)

        def body(t, ms):
            t_next = jnp.minimum(t + 1, ntrip - 1)
            pending = [st_ref[j] for j in range(SCORE_LOOKAHEAD)]
            new_ms = []
            for h in range(n_heads):
                g = h // heads_per_kv
                st = pending.pop(0)
                ahead = h + SCORE_LOOKAHEAD
                if ahead < n_heads:
                    pending.append(scores(t, ahead, shifted))
                else:
                    st_ref[ahead - n_heads] = scores(t_next, ahead - n_heads, shifted)
                if shifted:
                    p = jnp.exp2(st).astype(BF16)
                else:
                    m_new = jnp.maximum(ms[h], jnp.max(st, axis=0, keepdims=True))
                    alpha = jnp.exp2(ms[h] - m_new)
                    p = jnp.exp2(st - m_new).astype(BF16)
                    new_ms.append(m_new)
                pv = None
                for cc in range(CHUNKS_PER_TRIP):
                    vc = vt_ref[0, t * CHUNKS_PER_TRIP + cc, g * 64:(g + 1) * 64, :]
                    part = jnp.dot(jnp.concatenate([vc, ones_rows], axis=0), p[cc * tkc:(cc + 1) * tkc, :],
                                   preferred_element_type=F32)
                    pv = part if pv is None else pv + part
                rows = slice(h * ACC_ROWS, (h + 1) * ACC_ROWS)
                if shifted:
                    acc_ref[rows, :] = acc_ref[rows, :] + pv
                else:
                    acc_ref[rows, :] = alpha * acc_ref[rows, :] + pv
            return tuple(new_ms)

        init = () if shifted else (jnp.full((1, tq), -jnp.inf, F32),) * n_heads
        lax.fori_loop(0, ntrip, body, init)

    pl.when(shifted_ok)(functools.partial(attend, True))
    pl.when(jnp.logical_not(shifted_ok))(functools.partial(attend, False))

    for h in range(n_heads):
        num = acc_ref[h * ACC_ROWS:h * ACC_ROWS + 64, :]
        den = acc_ref[h * ACC_ROWS + 64:h * ACC_ROWS + 65, :]
        ot_ref[h * 64:(h + 1) * 64, :] = num * (1.0 / den)
    o_ref[0] = ot_ref[...].T.astype(BF16)


def _dense_call(qt, k, vt, kn, *, n_heads, heads_per_kv, dq, dk, k_stride, ones_lane, name):
    b, fq, s = qt.shape
    tq = TQ_DENSE
    fk = k.shape[2]
    _, nchunk, fv, tk = vt.shape
    nt = kn.shape[1]
    kern = functools.partial(_dense_kernel, n_heads=n_heads, heads_per_kv=heads_per_kv, dq=dq, dk=dk,
                             k_stride=k_stride, ones_lane=ones_lane)
    return pl.pallas_call(
        kern, grid=(b, s // tq),
        in_specs=[
            pl.BlockSpec((1, fq, tq), lambda bi, i: (bi, 0, i)),
            pl.BlockSpec((1, s, fk), lambda bi, i: (bi, 0, 0)),
            pl.BlockSpec((1, nchunk, fv, tk), lambda bi, i: (bi, 0, 0, 0)),
            pl.BlockSpec((1, nt, 1, LANE), lambda bi, i: (bi, 0, 0, 0)),
        ],
        out_specs=pl.BlockSpec((1, tq, n_heads * 64), lambda bi, i: (bi, i, 0)),
        out_shape=jax.ShapeDtypeStruct((b, s, n_heads * 64), BF16),
        scratch_shapes=[pltpu.VMEM((n_heads * ACC_ROWS, tq), F32), pltpu.VMEM((n_heads * 64, tq), F32),
                        pltpu.VMEM((SCORE_LOOKAHEAD, CHUNKS_PER_TRIP * tk, tq), F32),
                        pltpu.VMEM((n_heads * HEAD_PAD, tq), BF16)],
        compiler_params=pltpu.CompilerParams(dimension_semantics=("parallel", "arbitrary"),
                                             vmem_limit_bytes=VMEM_LIMIT),
        name=name,
    )(qt, k, vt, kn)


def _merge_kernel(x_ref, oa_ref, ob_ref, oc_ref, g1_ref, wg_ref, wa_ref, wb_ref, wc_ref, wo_ref,
                  g2_ref, wr_ref, br_ref, x1_ref, h2_ref, lg_ref):
    x = x_ref[...]
    ys = [jnp.dot(o_ref[...], w_ref[...], preferred_element_type=F32)
          for o_ref, w_ref in ((oa_ref, wa_ref), (ob_ref, wb_ref), (oc_ref, wc_ref))]
    hb = _rms(x, g1_ref[...]).astype(BF16)
    merged = None
    for j in range(N_BRANCHES):
        gl = jnp.dot(hb, wg_ref[:, j * D_MODEL:(j + 1) * D_MODEL], preferred_element_type=F32)
        t = jax.nn.sigmoid(gl) * ys[j]
        merged = t if merged is None else merged + t
    x1 = x + jnp.dot(merged.astype(BF16), wo_ref[...], preferred_element_type=F32)
    x1_ref[...] = x1
    h2 = _rms(x1, g2_ref[...])
    h2b = h2.astype(BF16)
    h2_ref[...] = h2b
    lg_ref[...] = jnp.dot(h2b, wr_ref[...], preferred_element_type=F32) + br_ref[...]


def _merge_call(x2d, oa, ob, oc, lw):
    t, d = x2d.shape
    tm = TM_MERGE
    const = lambda *shape: pl.BlockSpec(shape, lambda i: (0,) * len(shape))
    tok = lambda w: pl.BlockSpec((tm, w), lambda i: (i, 0))
    return pl.pallas_call(
        _merge_kernel, grid=(t // tm,),
        in_specs=[tok(d), tok(512), tok(512), tok(512), const(1, d), const(d, N_BRANCHES * d),
                  const(512, d), const(512, d), const(512, d), const(d, d), const(1, d),
                  const(d, ROUTER_COLS), const(1, ROUTER_COLS)],
        out_specs=[tok(d), tok(d), tok(ROUTER_COLS)],
        out_shape=[jax.ShapeDtypeStruct((t, d), F32), jax.ShapeDtypeStruct((t, d), BF16),
                   jax.ShapeDtypeStruct((t, ROUTER_COLS), F32)],
        compiler_params=pltpu.CompilerParams(dimension_semantics=("parallel",),
                                             vmem_limit_bytes=VMEM_LIMIT),
        name="merge",
    )(x2d, oa, ob, oc, lw["ln1_g"], lw["w_gate3"], lw["w_br_a"], lw["w_br_b"], lw["w_br_c"], lw["w_out"],
      lw["ln2_g"], lw["w_router"], lw["b_router"])


def _combine_weights(lg):
    lane = lax.broadcasted_iota(jnp.int32, lg.shape, 1)
    big = jnp.int32(1 << 20)
    ninf = jnp.float32(-jnp.inf)
    is_g = lane < N_GROUPS
    gl = jnp.where(is_g, lg, ninf)
    gmax = jnp.max(gl, axis=-1, keepdims=True)
    grp = jnp.min(jnp.where(is_g & (lg == gmax), lane, big), axis=-1, keepdims=True)
    gsum = jnp.sum(jnp.where(is_g, jnp.exp(gl - gmax), 0.0), axis=-1, keepdims=True)
    grp_w = 1.0 / gsum
    lo = EXPERT_LANE0 + grp * EXP_PER_GROUP
    in_grp = (lane >= lo) & (lane < lo + EXP_PER_GROUP)
    v1 = jnp.max(jnp.where(in_grp, lg, ninf), axis=-1, keepdims=True)
    i1 = jnp.min(jnp.where(in_grp & (lg == v1), lane, big), axis=-1, keepdims=True)
    rest = in_grp & (lane != i1)
    v2 = jnp.max(jnp.where(rest, lg, ninf), axis=-1, keepdims=True)
    i2 = jnp.min(jnp.where(rest & (lg == v2), lane, big), axis=-1, keepdims=True)
    e21 = jnp.exp(v2 - v1)
    den = 1.0 + e21
    w1 = (1.0 / den) * grp_w
    w2 = (e21 / den) * grp_w
    return jnp.where(lane == i1, w1, 0.0) + jnp.where(lane == i2, w2, 0.0), grp


def _moe_kernel(x1_ref, h2_ref, lg_ref, ltri_ref, wgu_ref, wd_ref, gf_ref, o_ref, xs_ref, cs_ref, ys_ref, *,
                final_norm):
    tm = h2_ref.shape[0]
    nblk = tm // MOE_BLOCK
    comb, grp = _combine_weights(lg_ref[...])
    lane = lax.broadcasted_iota(jnp.int32, comb.shape, 1)
    lane1 = lax.broadcasted_iota(jnp.int32, (1, ROUTER_COLS), 1)
    onehot = lane == grp
    cnt = jnp.sum(jnp.where(onehot, 1.0, 0.0), axis=0, keepdims=True)
    offs = [jnp.sum(jnp.where(lane1 < g, cnt, 0.0)) for g in range(N_GROUPS)]
    off_row = sum(jnp.where(lane1 == g, offs[g], 0.0) for g in range(N_GROUPS))
    rank = jnp.dot(ltri_ref[...], jnp.where(onehot, 1.0, 0.0).astype(BF16), preferred_element_type=F32)
    pos = jnp.sum(jnp.where(onehot, rank + off_row, 0.0), axis=1, keepdims=True).astype(jnp.int32)
    pos_row = jnp.where(lane == 0, pos, 0).astype(F32).T[0:1, :].astype(jnp.int32)
    perm = jnp.where(lax.broadcasted_iota(jnp.int32, (tm, tm), 0) == pos_row, 1.0, 0.0).astype(BF16)
    perm_t = jnp.where(lax.broadcasted_iota(jnp.int32, (tm, tm), 1) == pos, 1.0, 0.0).astype(BF16)

    c_hi = comb.astype(BF16)
    c_lo = (comb - c_hi.astype(F32)).astype(BF16)
    srt = jnp.dot(perm, jnp.concatenate([h2_ref[...], c_hi, c_lo], axis=1), preferred_element_type=F32)
    d = h2_ref.shape[1]
    xs_ref[...] = srt[:, 0:d].astype(BF16)
    cs_ref[...] = srt[:, d:d + ROUTER_COLS] + srt[:, d + ROUTER_COLS:d + 2 * ROUTER_COLS]
    ys_ref[...] = jnp.zeros(ys_ref.shape, F32)

    starts = [o.astype(jnp.int32) for o in offs]
    passes = []
    for b in range(nblk):
        g_first = sum((starts[g] <= b * MOE_BLOCK).astype(jnp.int32) for g in range(1, N_GROUPS))
        passes.append((jnp.int32(b), g_first, jnp.float32(1.0)))
    for g in range(1, N_GROUPS):
        inside = jnp.logical_and(starts[g] % MOE_BLOCK != 0, starts[g] < tm)
        passes.append((jnp.minimum(starts[g] // MOE_BLOCK, nblk - 1), jnp.int32(g), inside.astype(F32)))

    def gate_up(ps):
        b, g, _ = ps
        xb = xs_ref[pl.ds(pl.multiple_of(b * MOE_BLOCK, MOE_BLOCK), MOE_BLOCK), :]
        return [jnp.dot(xb, wgu_ref[g * EXP_PER_GROUP + j], preferred_element_type=F32)
                for j in range(EXP_PER_GROUP)]

    def finish(ps, gus):
        b, g, live = ps
        rows = pl.ds(pl.multiple_of(b * MOE_BLOCK, MOE_BLOCK), MOE_BLOCK)
        cb = cs_ref[rows, :] * live
        lane_b = lax.broadcasted_iota(jnp.int32, cb.shape, 1)
        acts = []
        for j in range(EXP_PER_GROUP):
            ce = jnp.sum(jnp.where(lane_b == EXPERT_LANE0 + g * EXP_PER_GROUP + j, cb, 0.0), axis=1,
                         keepdims=True)
            gt = gus[j][:, 0:D_EXPERT]
            acts.append((gt * jax.nn.sigmoid(gt) * gus[j][:, D_EXPERT:2 * D_EXPERT] * ce).astype(BF16))
        wrows = pl.ds(pl.multiple_of(g * (EXP_PER_GROUP * D_EXPERT), EXP_PER_GROUP * D_EXPERT),
                      EXP_PER_GROUP * D_EXPERT)
        y = jnp.dot(jnp.concatenate(acts, axis=1), wd_ref[wrows, :], preferred_element_type=F32)
        ys_ref[rows, :] = ys_ref[rows, :] + y

    pending = [gate_up(passes[0])]
    for i, ps in enumerate(passes):
        gus = pending.pop(0)
        if i + 1 < len(passes):
            pending.append(gate_up(passes[i + 1]))
        finish(ps, gus)

    ys = ys_ref[...]
    y_hi = ys.astype(BF16)
    y_lo = (ys - y_hi.astype(F32)).astype(BF16)
    y = x1_ref[...] + (jnp.dot(perm_t, y_hi, preferred_element_type=F32)
                       + jnp.dot(perm_t, y_lo, preferred_element_type=F32))
    if final_norm:
        y = _rms(y, gf_ref[...])
    o_ref[...] = y


def _moe_call(x1, h2, lg, lw, final_g, final_norm):
    t, d = x1.shape
    tm = TM_MOE
    tok = lambda w: pl.BlockSpec((tm, w), lambda i: (i, 0))
    once = pl.Buffered(1)
    idx = jnp.arange(tm)
    ltri = (idx[None, :] < idx[:, None]).astype(BF16)
    return pl.pallas_call(
        functools.partial(_moe_kernel, final_norm=final_norm), grid=(t // tm,),
        in_specs=[tok(d), tok(d), tok(ROUTER_COLS),
                  pl.BlockSpec((tm, tm), lambda i: (0, 0), pipeline_mode=once),
                  pl.BlockSpec((N_EXPERTS, d, 2 * D_EXPERT), lambda i: (0, 0, 0), pipeline_mode=once),
                  pl.BlockSpec((N_EXPERTS * D_EXPERT, d), lambda i: (0, 0), pipeline_mode=once),
                  pl.BlockSpec((1, d), lambda i: (0, 0))],
        out_specs=tok(d),
        out_shape=jax.ShapeDtypeStruct((t, d), F32),
        scratch_shapes=[pltpu.VMEM((tm, d), BF16), pltpu.VMEM((tm, ROUTER_COLS), F32),
                        pltpu.VMEM((tm, d), F32)],
        compiler_params=pltpu.CompilerParams(dimension_semantics=("parallel",),
                                             vmem_limit_bytes=VMEM_LIMIT),
        name="moe",
    )(x1, h2, lg, ltri, lw["w_gu"], lw["w_dn"], final_g)


def _layer_weights(l, w_in, ln1_g, sink_a, q_norm_g, w_uq, kv_norm_g, w_ukv, q_norm_c, k_norm_c,
                   w_br_a, w_br_b, w_br_c, w_out, ln2_g, w_grp, b_grp, w_exr, b_exr, w_gate, w_up, w_down):
    wi = w_in[l]
    d = wi.shape[0]
    z = lambda n: jnp.zeros((d, n), F32)
    w_att = jnp.concatenate([
        wi[:, 0:1152], z(64), wi[:, 1152:1184], z(32), wi[:, 1184:1696],
        wi[:, 1696:1760], z(64), wi[:, 1760:1824], z(64), wi[:, 1824:1952]], axis=1).astype(BF16)
    wq = w_uq[l].reshape(Q_LORA, HB, NOPE_B + ROPE_B)
    wq = jnp.concatenate([wq, jnp.zeros((Q_LORA, HB, HEAD_PAD - NOPE_B - ROPE_B), F32)], axis=-1)
    wkv = w_ukv[l].reshape(KV_LORA, HB, NOPE_B + V_B)
    wk = jnp.concatenate([wkv[..., :NOPE_B], jnp.zeros((KV_LORA, HB, HEAD_PAD - NOPE_B), F32)], axis=-1)
    zero64 = jnp.zeros((HD_C,), F32)
    w_router = jnp.concatenate([w_grp[l], w_exr[l], jnp.zeros((d, ROUTER_COLS - N_GROUPS - N_EXPERTS), F32)], 1)
    b_router = jnp.concatenate([b_grp[l], b_exr[l], jnp.zeros((ROUTER_COLS - N_GROUPS - N_EXPERTS,), F32)])
    return dict(
        ln1_g=ln1_g[l].reshape(1, d), w_att=w_att, w_gate3=wi[:, 1952:].astype(BF16),
        sink=jnp.repeat(sink_a[l] * LOG2E, TQ_WINDOW).reshape(HA_Q // 2, 1, 2 * TQ_WINDOW), q_norm_g=q_norm_g[l].reshape(1, Q_LORA),
        w_uq=wq.reshape(Q_LORA, HB * HEAD_PAD).astype(BF16),
        kv_norm_g=kv_norm_g[l].reshape(1, KV_LORA),
        w_uk=wk.reshape(KV_LORA, HB * HEAD_PAD).astype(BF16),
        w_uv=wkv[..., NOPE_B:].reshape(KV_LORA, HB * V_B).astype(BF16),
        gqc=jnp.tile(q_norm_c[l], HC_Q).reshape(1, 512),
        gkc=jnp.concatenate([k_norm_c[l], zero64, k_norm_c[l], zero64]).reshape(1, 256),
        w_br_a=w_br_a[l].astype(BF16), w_br_b=w_br_b[l].astype(BF16), w_br_c=w_br_c[l].astype(BF16),
        w_out=w_out[l].astype(BF16), ln2_g=ln2_g[l].reshape(1, d),
        w_router=w_router.astype(BF16), b_router=b_router.reshape(1, ROUTER_COLS),
        w_gu=jnp.concatenate([w_gate[l], w_up[l]], axis=-1).astype(BF16),
        w_dn=w_down[l].reshape(N_EXPERTS * D_EXPERT, d).astype(BF16),
    )


def _t5_bucket(rel):
    nb = N_BUCKETS // 2
    bucket = jnp.where(rel > 0, nb, 0)
    n = jnp.abs(rel)
    max_exact = nb // 2
    nf = jnp.maximum(n, 1).astype(F32)
    large = max_exact + (jnp.log(nf / max_exact) / math.log(MAX_DISTANCE / max_exact)
                         * (nb - max_exact)).astype(jnp.int32)
    large = jnp.minimum(large, nb - 1)
    return bucket + jnp.where(n < max_exact, n, large)


def _tables(s, rel_bias):
    half = ROPE_B // 2
    inv = ROPE_THETA ** (-jnp.arange(half, dtype=F32) / half)
    pos = jnp.arange(s, dtype=jnp.int32)
    row = (pos // GRID_W).astype(F32)
    col = (pos % GRID_W).astype(F32)

    def cs(p):
        ang = p[:, None] * inv[None, :]
        return jnp.cos(ang), jnp.sin(ang)

    cp, sp = cs(pos.astype(F32))
    cr, sr = cs(row)
    cc, sc = cs(col)
    one = jnp.ones((s, 64), F32)
    z32 = jnp.zeros((s, 32), F32)
    z64 = jnp.zeros((s, 64), F32)
    b_cos = jnp.concatenate([one, cp, cp, z32], axis=1)
    b_sin = jnp.concatenate([z64, -sp, sp, z32], axis=1)
    c_cos = jnp.concatenate([cr, cr, cc, cc], axis=1)
    c_sin = jnp.concatenate([-sr, sr, -sc, sc], axis=1)
    scale_b = (NOPE_B + ROPE_B) ** -0.5 * LOG2E
    scale_c = HD_C ** -0.5 * LOG2E
    gidx = jnp.arange(512) // HD_C
    gmat = (gidx[:, None] == gidx[None, :]).astype(BF16)
    period = 4 * TQ_WINDOW + 1
    mats = []
    for shift in (0, WINDOW, 2 * WINDOW):
        rel = jnp.arange(period) - TQ_WINDOW - shift
        prof = jnp.where((jnp.abs(rel) <= WINDOW)[:, None], rel_bias[_t5_bucket(rel)].astype(F32) * LOG2E,
                         NEG_INF).T
        skew = jnp.tile(prof, (1, TQ_WINDOW))[:, :TQ_WINDOW * (period - 1)]
        skew = skew.reshape(HA_Q, TQ_WINDOW, period - 1)[:, :, TQ_WINDOW:TQ_WINDOW + TK_WINDOW]
        bias = jnp.transpose(skew, (0, 2, 1)).reshape(HA_Q // 2, 2, TK_WINDOW, TQ_WINDOW)
        mats.append(jnp.transpose(bias, (0, 2, 1, 3)).reshape(HA_Q // 2, TK_WINDOW, 2 * TQ_WINDOW))
    return dict(
        bqc=b_cos * scale_b, bqs=b_sin * scale_b, bkc=b_cos, bks=b_sin,
        cqc=jnp.concatenate([c_cos, c_cos], 1) * scale_c, cqs=jnp.concatenate([c_sin, c_sin], 1) * scale_c,
        ckc=jnp.concatenate([c_cos, z64], 1), cks=jnp.concatenate([c_sin, z64], 1),
        gmat=gmat, bias3=jnp.stack(mats, axis=0), **_key_norm_constants())


def _key_norm_constants():
    lb = jnp.arange(HB * HEAD_PAD)
    ghb = ((lb[:, None] // HEAD_PAD == jnp.arange(LANE)[None, :]) & (lb[:, None] % HEAD_PAD < ONES_LANE_B))
    lc = jnp.arange(HC_KV * LANE)
    ghc = ((lc[:, None] // LANE == jnp.arange(LANE)[None, :]) & (lc[:, None] % LANE < ONES_LANE_C))
    return dict(ghb=ghb.astype(BF16), ghc=ghc.astype(BF16),
                oneb=(lb % HEAD_PAD == ONES_LANE_B).astype(F32).reshape(1, -1),
                onec=(lc % LANE == ONES_LANE_C).astype(F32).reshape(1, -1))


def _trunk(x, layers, tabs, final_g):
    b, s, d = x.shape
    n_layers = len(layers)
    for l, lw in enumerate(layers):
        qa, ka, va, qbt, kb, vbt, qct, kc, vct, kbn, kcn = _prep_call(x, lw, tabs)
        oa = _window_call(qa, ka, va, lw["sink"], tabs["bias3"])
        ob = _dense_call(qbt, kb, vbt, kbn, n_heads=HB, heads_per_kv=1, dq=HEAD_PAD, dk=HEAD_PAD,
                         k_stride=HEAD_PAD, ones_lane=ONES_LANE_B, name="latent_attn")
        oc = _dense_call(qct, kc, vct, kcn, n_heads=HC_Q, heads_per_kv=HC_Q // HC_KV, dq=HD_C, dk=HD_C,
                         k_stride=LANE, ones_lane=ONES_LANE_C, name="axial_attn")
        t = b * s
        x1, h2, lg = _merge_call(x.reshape(t, d), oa.reshape(t, 512), ob.reshape(t, 512),
                                 oc.reshape(t, 512), lw)
        x = _moe_call(x1, h2, lg, lw, final_g, l == n_layers - 1).reshape(b, s, d)
    return x


def kernel(x_prompt, x_sample, rel_bias, final_g, ln1_g, w_in, sink_a, q_norm_g, w_uq, kv_norm_g, w_ukv,
           q_norm_c, k_norm_c, w_br_a, w_br_b, w_br_c, w_out, ln2_g, w_grp, b_grp, w_exr, b_exr, w_gate,
           w_up, w_down):
    n_layers = w_in.shape[0]
    layers = [_layer_weights(l, w_in, ln1_g, sink_a, q_norm_g, w_uq, kv_norm_g, w_ukv, q_norm_c, k_norm_c,
                             w_br_a, w_br_b, w_br_c, w_out, ln2_g, w_grp, b_grp, w_exr, b_exr, w_gate,
                             w_up, w_down) for l in range(n_layers)]
    fg = final_g.reshape(1, -1)
    outs = []
    for x in (x_prompt, x_sample):
        tabs = _tables(x.shape[1], rel_bias)
        outs.append(_trunk(x, layers, tabs, fg))
    return tuple(outs)
```

```python
import functools
import math

import jax
import jax.numpy as jnp
from jax import lax
from jax.experimental import pallas as pl
from jax.experimental.pallas import tpu as pltpu

F32 = jnp.float32
BF16 = jnp.bfloat16

D_MODEL = 1024
GRID_W = 64
BLOCK = 128
EPS = 1e-6
NEG_INF = -1e30
ROPE_THETA = 10000.0
HA_Q, HA_KV, HD_A = 8, 2, 64
WINDOW = 128
N_BUCKETS = 32
MAX_DISTANCE = 128
HB, Q_LORA, KV_LORA, NOPE_B, ROPE_B, V_B = 8, 256, 128, 64, 32, 64
HC_Q, HC_KV, HD_C = 8, 2, 64
N_BRANCHES = 3
N_GROUPS, EXP_PER_GROUP = 4, 4
N_EXPERTS = N_GROUPS * EXP_PER_GROUP
D_EXPERT = 256

LANE = 128
HEAD_PAD = 128
ATT_COLS = 2176
ROUTER_COLS = 128
GROUP_LANE0 = 0
EXPERT_LANE0 = N_GROUPS

TM_PREP = 256
TK_DENSE = TM_PREP
TQ_DENSE = 1024
TQ_WINDOW = 256
TK_WINDOW = TQ_WINDOW + 2 * WINDOW
WINDOW_SUBTILES = 2
CHUNKS_PER_TRIP = 1
SCORE_LOOKAHEAD = 2
SCORE_SLOTS = 4
ACC_ROWS = 80
LOG2E = math.log2(math.e)
ONES_LANE_B = NOPE_B + ROPE_B
ONES_LANE_C = HD_C
BOUND_SLACK = 1.02
SHIFT_BOUND_MAX = 48.0
TM_MERGE = 256
TM_MOE = 512
MOE_BLOCK = 128
VMEM_LIMIT = 56 * 1024 * 1024


def _rms(x, g):
    return x * lax.rsqrt(jnp.mean(x * x, axis=-1, keepdims=True) + EPS) * g


def _swap16(x):
    n = x.shape[-1]
    lane = lax.broadcasted_iota(jnp.int32, x.shape, x.ndim - 1)
    return jnp.where((lane & 16) == 0, pltpu.roll(x, n - 16, x.ndim - 1), pltpu.roll(x, 16, x.ndim - 1))


def _tile_lanes(t, reps):
    return jnp.concatenate([t] * reps, axis=-1) if reps > 1 else t


def _group_ssq(x, g_ref):
    sq = x * x
    hi = sq.astype(BF16)
    lo = (sq - hi.astype(F32)).astype(BF16)
    g = g_ref[...]
    return (jnp.dot(hi, g, preferred_element_type=F32) + jnp.dot(lo, g, preferred_element_type=F32))


def _max_sq_norm(k, g_ref):
    kr = k.astype(BF16).astype(F32)
    sq = jnp.dot((kr * kr).astype(BF16), g_ref[...], preferred_element_type=F32)
    return jnp.max(sq, axis=0, keepdims=True)


def _prep_kernel(x_ref, g1_ref, w_ref, qng_ref, wuq_ref, kvng_ref, wuk_ref, wuv_ref,
                 gqc_ref, gkc_ref, gmat_ref, ghb_ref, ghc_ref, oneb_ref, onec_ref,
                 bqc_ref, bqs_ref, bkc_ref, bks_ref, cqc_ref, cqs_ref, ckc_ref, cks_ref,
                 qa_ref, ka_ref, va_ref, qbt_ref, kb_ref, vbt_ref, qct_ref, kc_ref, vct_ref,
                 kbn_ref, kcn_ref):
    x = x_ref[0]
    hb = _rms(x, g1_ref[...]).astype(BF16)

    pb = jnp.dot(hb, w_ref[:, 768:1280], preferred_element_type=F32)
    pc = jnp.dot(hb, w_ref[:, 1280:2176], preferred_element_type=F32)
    pa = jnp.dot(hb, w_ref[:, 0:768], preferred_element_type=F32)
    cq = _rms(pb[:, 0:256], qng_ref[...]).astype(BF16)
    ckv = _rms(pb[:, 256:384], kvng_ref[...]).astype(BF16)
    qb = jnp.dot(cq, wuq_ref[...], preferred_element_type=F32)
    kb = jnp.dot(ckv, wuk_ref[...], preferred_element_type=F32)
    vb = jnp.dot(ckv, wuv_ref[...], preferred_element_type=F32)
    qc = pc[:, 0:512]
    kc = pc[:, 512:768]
    ssq_q = _group_ssq(qc, gmat_ref)
    ssq_k = _group_ssq(kc, gmat_ref.at[0:256, 0:256])

    qa_ref[0] = (pa[:, 0:512] * (HD_A ** -0.5 * LOG2E)).T.astype(BF16)
    ka_ref[0] = pa[:, 512:640].astype(BF16)
    vat = pa[:, 640:768].T.astype(BF16)
    for r in range(TM_PREP // BLOCK):
        va_ref[0, r] = vat[:, r * BLOCK:(r + 1) * BLOCK]

    qb = qb * _tile_lanes(bqc_ref[...], HB) + _swap16(qb) * _tile_lanes(bqs_ref[...], HB)
    qbt_ref[0] = qb.T.astype(BF16)
    kpe = pb[:, 384:512]
    kpe = kpe * bkc_ref[...] + _swap16(kpe) * bks_ref[...]
    kb = kb + _tile_lanes(kpe, HB)
    kbn_ref[0, 0] = _max_sq_norm(kb, ghb_ref)
    kb_ref[0] = (kb + oneb_ref[...]).astype(BF16)
    vbt_ref[0, 0] = vb.T.astype(BF16)

    qc = qc * lax.rsqrt(ssq_q * (1.0 / HD_C) + EPS) * gqc_ref[...]
    qc = qc * _tile_lanes(cqc_ref[...], 4) + _swap16(qc) * _tile_lanes(cqs_ref[...], 4)
    qct_ref[0] = qc.T.astype(BF16)
    kc = kc * lax.rsqrt(ssq_k * (1.0 / HD_C) + EPS) * gkc_ref[...]
    kc = kc * _tile_lanes(ckc_ref[...], 2) + _swap16(kc) * _tile_lanes(cks_ref[...], 2)
    kcn_ref[0, 0] = _max_sq_norm(kc, ghc_ref)
    kc_ref[0] = (kc + onec_ref[...]).astype(BF16)
    vct_ref[0, 0] = pc[:, 768:896].T.astype(BF16)


def _prep_call(x, lw, tabs):
    b, s, d = x.shape
    tm = TM_PREP
    nt = s // tm
    const = lambda *shape: pl.BlockSpec(shape, lambda bi, ti: (0,) * len(shape))
    tab = lambda w: pl.BlockSpec((tm, w), lambda bi, ti: (ti, 0))
    in_specs = [
        pl.BlockSpec((1, tm, d), lambda bi, ti: (bi, ti, 0)),
        const(1, d), const(d, ATT_COLS), const(1, Q_LORA), const(Q_LORA, HB * HEAD_PAD),
        const(1, KV_LORA), const(KV_LORA, HB * HEAD_PAD), const(KV_LORA, HB * V_B),
        const(1, 512), const(1, 256), const(512, 512),
        const(HB * HEAD_PAD, LANE), const(256, LANE), const(1, HB * HEAD_PAD), const(1, 256),
        tab(LANE), tab(LANE), tab(LANE), tab(LANE), tab(LANE), tab(LANE), tab(LANE), tab(LANE),
    ]
    tok = lambda w: pl.BlockSpec((1, tm, w), lambda bi, ti: (bi, ti, 0))
    tr = lambda w: pl.BlockSpec((1, w, tm), lambda bi, ti: (bi, 0, ti))
    chunked = lambda w: pl.BlockSpec((1, 1, w, tm), lambda bi, ti: (bi, ti, 0, 0))
    blocks = pl.BlockSpec((1, tm // BLOCK, 128, BLOCK), lambda bi, ti: (bi, ti, 0, 0))
    tile_row = pl.BlockSpec((1, 1, 1, LANE), lambda bi, ti: (bi, ti, 0, 0))
    out_specs = [tr(512), tok(128), blocks, tr(HB * HEAD_PAD), tok(HB * HEAD_PAD), chunked(HB * V_B),
                 tr(512), tok(256), chunked(128), tile_row, tile_row]
    out_shape = [
        jax.ShapeDtypeStruct((b, 512, s), BF16), jax.ShapeDtypeStruct((b, s, 128), BF16),
        jax.ShapeDtypeStruct((b, s // BLOCK, 128, BLOCK), BF16),
        jax.ShapeDtypeStruct((b, HB * HEAD_PAD, s), BF16), jax.ShapeDtypeStruct((b, s, HB * HEAD_PAD), BF16),
        jax.ShapeDtypeStruct((b, nt, HB * V_B, tm), BF16),
        jax.ShapeDtypeStruct((b, 512, s), BF16), jax.ShapeDtypeStruct((b, s, 256), BF16),
        jax.ShapeDtypeStruct((b, nt, 128, tm), BF16),
        jax.ShapeDtypeStruct((b, nt, 1, LANE), F32), jax.ShapeDtypeStruct((b, nt, 1, LANE), F32),
    ]
    return pl.pallas_call(
        _prep_kernel, grid=(b, nt), in_specs=in_specs, out_specs=out_specs, out_shape=out_shape,
        compiler_params=pltpu.CompilerParams(dimension_semantics=("parallel", "parallel"),
                                             vmem_limit_bytes=VMEM_LIMIT),
        name="prep",
    )(x, lw["ln1_g"], lw["w_att"], lw["q_norm_g"], lw["w_uq"], lw["kv_norm_g"], lw["w_uk"], lw["w_uv"],
      lw["gqc"], lw["gkc"], tabs["gmat"], tabs["ghb"], tabs["ghc"], tabs["oneb"], tabs["onec"],
      tabs["bqc"], tabs["bqs"], tabs["bkc"], tabs["bks"], tabs["cqc"], tabs["cqs"], tabs["ckc"], tabs["cks"])


def _window_kernel(sink_ref, qt_ref, k_ref, vt_ref, bias0_ref, bias1_ref, o_ref, ot_ref, st_ref):
    nblk = vt_ref.shape[1]
    wblk = TK_WINDOW // BLOCK
    grp = HA_Q // HA_KV
    npair = HA_Q // 2
    bias_refs = (bias0_ref, bias1_ref)
    ones_rows = (lax.broadcasted_iota(jnp.int32, (ACC_ROWS - HD_A, TK_WINDOW), 0) == 0).astype(BF16)

    kgs, vexts = [], []
    for sub in range(WINDOW_SUBTILES):
        i = pl.program_id(1) * WINDOW_SUBTILES + sub
        cs = jnp.clip(i * (TQ_WINDOW // BLOCK) - 1, 0, nblk - wblk)
        kwin = k_ref[0, pl.ds(pl.multiple_of(cs * BLOCK, BLOCK), TK_WINDOW), :]
        for g in range(HA_KV):
            kgs.append(kwin[:, g * HD_A:(g + 1) * HD_A])
            vwin = jnp.concatenate([vt_ref[0, cs + r, g * HD_A:(g + 1) * HD_A, :] for r in range(wblk)],
                                   axis=1)
            vexts.append(jnp.concatenate([vwin, ones_rows], axis=0))

    def scores(n):
        sub, c = divmod(n, npair)
        cols = slice(sub * TQ_WINDOW, (sub + 1) * TQ_WINDOW)
        qt = jnp.concatenate([qt_ref[0, h * HD_A:(h + 1) * HD_A, cols] for h in (2 * c, 2 * c + 1)], axis=1)
        return jnp.dot(kgs[sub * HA_KV + 2 * c // grp], qt, preferred_element_type=F32)

    nchain = WINDOW_SUBTILES * npair
    for n in range(SCORE_LOOKAHEAD):
        st_ref[n] = scores(n)
    for n in range(nchain):
        sub, c = divmod(n, npair)
        if n + SCORE_LOOKAHEAD < nchain:
            st_ref[(n + SCORE_LOOKAHEAD) % SCORE_SLOTS] = scores(n + SCORE_LOOKAHEAD)
        st = st_ref[n % SCORE_SLOTS] + bias_refs[sub][0, c]
        sink = sink_ref[c]
        m = jnp.maximum(jnp.max(st, axis=0, keepdims=True), sink)
        p = jnp.exp2(st - m).astype(BF16)
        pv = jnp.dot(vexts[sub * HA_KV + 2 * c // grp], p, preferred_element_type=F32)
        o2 = pv[0:HD_A, :] / (pv[HD_A:HD_A + 1, :] + jnp.exp2(sink - m))
        for r in range(2):
            h = 2 * c + r
            ot_ref[h * HD_A:(h + 1) * HD_A, sub * TQ_WINDOW:(sub + 1) * TQ_WINDOW] = (
                o2[:, r * TQ_WINDOW:(r + 1) * TQ_WINDOW])
    o_ref[0] = ot_ref[...].T.astype(BF16)


def _window_call(qat, ka, vat, sink, bias3):
    b, _, s = qat.shape
    nq = s // TQ_WINDOW
    tstep = WINDOW_SUBTILES * TQ_WINDOW

    def variant(sub):
        def index(bi, i):
            t = i * WINDOW_SUBTILES + sub
            return (jnp.where(t == 0, 0, jnp.where(t == nq - 1, 2, 1)), 0, 0, 0)
        return pl.BlockSpec((1, HA_Q // 2, TK_WINDOW, 2 * TQ_WINDOW), index)

    return pl.pallas_call(
        _window_kernel, grid=(b, s // tstep),
        in_specs=[
            pl.BlockSpec((HA_Q // 2, 1, 2 * TQ_WINDOW), lambda bi, i: (0, 0, 0)),
            pl.BlockSpec((1, 512, tstep), lambda bi, i: (bi, 0, i)),
            pl.BlockSpec((1, s, 128), lambda bi, i: (bi, 0, 0)),
            pl.BlockSpec((1, s // BLOCK, 128, BLOCK), lambda bi, i: (bi, 0, 0, 0)),
            variant(0), variant(1),
        ],
        out_specs=pl.BlockSpec((1, tstep, 512), lambda bi, i: (bi, i, 0)),
        out_shape=jax.ShapeDtypeStruct((b, s, 512), BF16),
        scratch_shapes=[pltpu.VMEM((HA_Q * HD_A, tstep), F32),
                        pltpu.VMEM((SCORE_SLOTS, TK_WINDOW, 2 * TQ_WINDOW), F32)],
        compiler_params=pltpu.CompilerParams(dimension_semantics=("parallel", "arbitrary"),
                                             vmem_limit_bytes=VMEM_LIMIT),
        name="window_attn",
    )(sink, qat, ka, vat, bias3, bias3)


def _dense_kernel(qt_ref, k_ref, vt_ref, kn_ref, o_ref, acc_ref, ot_ref, st_ref, qs_ref, *, n_heads,
                  heads_per_kv, dq, dk, k_stride, ones_lane):
    nchunk = vt_ref.shape[1]
    tkc = vt_ref.shape[3]
    tq = qt_ref.shape[2]
    tk = CHUNKS_PER_TRIP * tkc
    ntrip = nchunk // CHUNKS_PER_TRIP
    ones_rows = (lax.broadcasted_iota(jnp.int32, (ACC_ROWS - 64, tkc), 0) == 0).astype(BF16)

    kn2 = jnp.max(kn_ref[0, :, 0, :], axis=0, keepdims=True)
    lane = lax.broadcasted_iota(jnp.int32, kn2.shape, 1)
    row = lax.broadcasted_iota(jnp.int32, (HEAD_PAD, tq), 0)
    bound = None
    for h in range(n_heads):
        g = h // heads_per_kv
        q = qt_ref[0, h * dq:h * dq + dk, :]
        if dk < HEAD_PAD:
            q = jnp.concatenate([q, jnp.zeros((HEAD_PAD - dk, tq), BF16)], axis=0)
        qf = q.astype(F32)
        kmax2 = jnp.sum(jnp.where(lane == g, kn2, 0.0), axis=1, keepdims=True)
        u = jnp.sqrt(jnp.sum(qf * qf, axis=0, keepdims=True) * kmax2) * BOUND_SLACK
        qs_ref[h * HEAD_PAD:(h + 1) * HEAD_PAD, :] = jnp.where(row == ones_lane, -u, qf).astype(BF16)
        hmax = jnp.max(u)
        bound = hmax if bound is None else jnp.maximum(bound, hmax)
    shifted_ok = bound <= SHIFT_BOUND_MAX

    def scores(t, h, shifted):
        g = h // heads_per_kv
        off = pl.multiple_of(t * tk, tk)
        if shifted:
            qt = qs_ref[h * HEAD_PAD:(h + 1) * HEAD_PAD, :]
            kc = k_ref[0, pl.ds(off, tk), g * k_stride:g * k_stride + HEAD_PAD]
        else:
            qt = qt_ref[0, h * dq:h * dq + dk, :]
            kc = k_ref[0, pl.ds(off, tk), g * k_stride:g * k_stride + dk]
        return jnp.dot(kc, qt, preferred_element_type=F32)

    def attend(shifted):
        acc_ref[...] = jnp.zeros(acc_ref.shape, F32)
        for j in range(SCORE_LOOKAHEAD):
            st_ref[j] = scores(0, j, shifted)

        def body(t, ms):
            t_next = jnp.minimum(t + 1, ntrip - 1)
            new_ms = []
            for h in range(n_heads):
                g = h // heads_per_kv
                ahead = h + SCORE_LOOKAHEAD
                if ahead < n_heads:
                    st_ref[ahead % SCORE_SLOTS] = scores(t, ahead, shifted)
                else:
                    st_ref[ahead % SCORE_SLOTS] = scores(t_next, ahead - n_heads, shifted)
                st = st_ref[h % SCORE_SLOTS]
                if shifted:
                    p = jnp.exp2(st).astype(BF16)
                else:
                    m_new = jnp.maximum(ms[h], jnp.max(st, axis=0, keepdims=True))
                    alpha = jnp.exp2(ms[h] - m_new)
                    p = jnp.exp2(st - m_new).astype(BF16)
                    new_ms.append(m_new)
                pv = None
                for cc in range(CHUNKS_PER_TRIP):
                    vc = vt_ref[0, t * CHUNKS_PER_TRIP + cc, g * 64:(g + 1) * 64, :]
                    part = jnp.dot(jnp.concatenate([vc, ones_rows], axis=0), p[cc * tkc:(cc + 1) * tkc, :],
                                   preferred_element_type=F32)
                    pv = part if pv is None else pv + part
                rows = slice(h * ACC_ROWS, (h + 1) * ACC_ROWS)
                if shifted:
                    acc_ref[rows, :] = acc_ref[rows, :] + pv
                else:
                    acc_ref[rows, :] = alpha * acc_ref[rows, :] + pv
            return tuple(new_ms)

        init = () if shifted else (jnp.full((1, tq), -jnp.inf, F32),) * n_heads
        lax.fori_loop(0, ntrip, body, init)

    pl.when(shifted_ok)(functools.partial(attend, True))
    pl.when(jnp.logical_not(shifted_ok))(functools.partial(attend, False))

    for h in range(n_heads):
        num = acc_ref[h * ACC_ROWS:h * ACC_ROWS + 64, :]
        den = acc_ref[h * ACC_ROWS + 64:h * ACC_ROWS + 65, :]
        ot_ref[h * 64:(h + 1) * 64, :] = num * (1.0 / den)
    o_ref[0] = ot_ref[...].T.astype(BF16)


def _dense_call(qt, k, vt, kn, *, n_heads, heads_per_kv, dq, dk, k_stride, ones_lane, name):
    b, fq, s = qt.shape
    tq = TQ_DENSE
    fk = k.shape[2]
    _, nchunk, fv, tk = vt.shape
    nt = kn.shape[1]
    kern = functools.partial(_dense_kernel, n_heads=n_heads, heads_per_kv=heads_per_kv, dq=dq, dk=dk,
                             k_stride=k_stride, ones_lane=ones_lane)
    return pl.pallas_call(
        kern, grid=(b, s // tq),
        in_specs=[
            pl.BlockSpec((1, fq, tq), lambda bi, i: (bi, 0, i)),
            pl.BlockSpec((1, s, fk), lambda bi, i: (bi, 0, 0)),
            pl.BlockSpec((1, nchunk, fv, tk), lambda bi, i: (bi, 0, 0, 0)),
            pl.BlockSpec((1, nt, 1, LANE), lambda bi, i: (bi, 0, 0, 0)),
        ],
        out_specs=pl.BlockSpec((1, tq, n_heads * 64), lambda bi, i: (bi, i, 0)),
        out_shape=jax.ShapeDtypeStruct((b, s, n_heads * 64), BF16),
        scratch_shapes=[pltpu.VMEM((n_heads * ACC_ROWS, tq), F32), pltpu.VMEM((n_heads * 64, tq), F32),
                        pltpu.VMEM((SCORE_SLOTS, CHUNKS_PER_TRIP * tk, tq), F32),
                        pltpu.VMEM((n_heads * HEAD_PAD, tq), BF16)],
        compiler_params=pltpu.CompilerParams(dimension_semantics=("parallel", "arbitrary"),
                                             vmem_limit_bytes=VMEM_LIMIT),
        name=name,
    )(qt, k, vt, kn)


def _merge_kernel(x_ref, oa_ref, ob_ref, oc_ref, g1_ref, wg_ref, wa_ref, wb_ref, wc_ref, wo_ref,
                  g2_ref, wr_ref, br_ref, x1_ref, h2_ref, lg_ref):
    x = x_ref[...]
    ys = [jnp.dot(o_ref[...], w_ref[...], preferred_element_type=F32)
          for o_ref, w_ref in ((oa_ref, wa_ref), (ob_ref, wb_ref), (oc_ref, wc_ref))]
    hb = _rms(x, g1_ref[...]).astype(BF16)
    merged = None
    for j in range(N_BRANCHES):
        gl = jnp.dot(hb, wg_ref[:, j * D_MODEL:(j + 1) * D_MODEL], preferred_element_type=F32)
        t = jax.nn.sigmoid(gl) * ys[j]
        merged = t if merged is None else merged + t
    x1 = x + jnp.dot(merged.astype(BF16), wo_ref[...], preferred_element_type=F32)
    x1_ref[...] = x1
    h2 = _rms(x1, g2_ref[...])
    h2b = h2.astype(BF16)
    h2_ref[...] = h2b
    lg_ref[...] = jnp.dot(h2b, wr_ref[...], preferred_element_type=F32) + br_ref[...]


def _merge_call(x2d, oa, ob, oc, lw):
    t, d = x2d.shape
    tm = TM_MERGE
    const = lambda *shape: pl.BlockSpec(shape, lambda i: (0,) * len(shape))
    tok = lambda w: pl.BlockSpec((tm, w), lambda i: (i, 0))
    return pl.pallas_call(
        _merge_kernel, grid=(t // tm,),
        in_specs=[tok(d), tok(512), tok(512), tok(512), const(1, d), const(d, N_BRANCHES * d),
                  const(512, d), const(512, d), const(512, d), const(d, d), const(1, d),
                  const(d, ROUTER_COLS), const(1, ROUTER_COLS)],
        out_specs=[tok(d), tok(d), tok(ROUTER_COLS)],
        out_shape=[jax.ShapeDtypeStruct((t, d), F32), jax.ShapeDtypeStruct((t, d), BF16),
                   jax.ShapeDtypeStruct((t, ROUTER_COLS), F32)],
        compiler_params=pltpu.CompilerParams(dimension_semantics=("parallel",),
                                             vmem_limit_bytes=VMEM_LIMIT),
        name="merge",
    )(x2d, oa, ob, oc, lw["ln1_g"], lw["w_gate3"], lw["w_br_a"], lw["w_br_b"], lw["w_br_c"], lw["w_out"],
      lw["ln2_g"], lw["w_router"], lw["b_router"])


def _combine_weights(lg):
    lane = lax.broadcasted_iota(jnp.int32, lg.shape, 1)
    big = jnp.int32(1 << 20)
    ninf = jnp.float32(-jnp.inf)
    is_g = lane < N_GROUPS
    gl = jnp.where(is_g, lg, ninf)
    gmax = jnp.max(gl, axis=-1, keepdims=True)
    grp = jnp.min(jnp.where(is_g & (lg == gmax), lane, big), axis=-1, keepdims=True)
    gsum = jnp.sum(jnp.where(is_g, jnp.exp(gl - gmax), 0.0), axis=-1, keepdims=True)
    grp_w = 1.0 / gsum
    lo = EXPERT_LANE0 + grp * EXP_PER_GROUP
    in_grp = (lane >= lo) & (lane < lo + EXP_PER_GROUP)
    v1 = jnp.max(jnp.where(in_grp, lg, ninf), axis=-1, keepdims=True)
    i1 = jnp.min(jnp.where(in_grp & (lg == v1), lane, big), axis=-1, keepdims=True)
    rest = in_grp & (lane != i1)
    v2 = jnp.max(jnp.where(rest, lg, ninf), axis=-1, keepdims=True)
    i2 = jnp.min(jnp.where(rest & (lg == v2), lane, big), axis=-1, keepdims=True)
    e21 = jnp.exp(v2 - v1)
    den = 1.0 + e21
    w1 = (1.0 / den) * grp_w
    w2 = (e21 / den) * grp_w
    return jnp.where(lane == i1, w1, 0.0) + jnp.where(lane == i2, w2, 0.0), grp


def _moe_kernel(x1_ref, h2_ref, lg_ref, ltri_ref, wgu_ref, wd_ref, gf_ref, o_ref, xs_ref, cs_ref, ys_ref, *,
                final_norm):
    tm = h2_ref.shape[0]
    nblk = tm // MOE_BLOCK
    comb, grp = _combine_weights(lg_ref[...])
    lane = lax.broadcasted_iota(jnp.int32, comb.shape, 1)
    lane1 = lax.broadcasted_iota(jnp.int32, (1, ROUTER_COLS), 1)
    onehot = lane == grp
    cnt = jnp.sum(jnp.where(onehot, 1.0, 0.0), axis=0, keepdims=True)
    offs = [jnp.sum(jnp.where(lane1 < g, cnt, 0.0)) for g in range(N_GROUPS)]
    off_row = sum(jnp.where(lane1 == g, offs[g], 0.0) for g in range(N_GROUPS))
    rank = jnp.dot(ltri_ref[...], jnp.where(onehot, 1.0, 0.0).astype(BF16), preferred_element_type=F32)
    pos = jnp.sum(jnp.where(onehot, rank + off_row, 0.0), axis=1, keepdims=True).astype(jnp.int32)
    pos_row = jnp.where(lane == 0, pos, 0).astype(F32).T[0:1, :].astype(jnp.int32)
    perm = jnp.where(lax.broadcasted_iota(jnp.int32, (tm, tm), 0) == pos_row, 1.0, 0.0).astype(BF16)
    perm_t = jnp.where(lax.broadcasted_iota(jnp.int32, (tm, tm), 1) == pos, 1.0, 0.0).astype(BF16)

    c_hi = comb.astype(BF16)
    c_lo = (comb - c_hi.astype(F32)).astype(BF16)
    srt = jnp.dot(perm, jnp.concatenate([h2_ref[...], c_hi, c_lo], axis=1), preferred_element_type=F32)
    d = h2_ref.shape[1]
    xs_ref[...] = srt[:, 0:d].astype(BF16)
    cs_ref[...] = srt[:, d:d + ROUTER_COLS] + srt[:, d + ROUTER_COLS:d + 2 * ROUTER_COLS]
    ys_ref[...] = jnp.zeros(ys_ref.shape, F32)

    starts = [o.astype(jnp.int32) for o in offs]
    passes = []
    for b in range(nblk):
        g_first = sum((starts[g] <= b * MOE_BLOCK).astype(jnp.int32) for g in range(1, N_GROUPS))
        passes.append((jnp.int32(b), g_first, jnp.float32(1.0)))
    for g in range(1, N_GROUPS):
        inside = jnp.logical_and(starts[g] % MOE_BLOCK != 0, starts[g] < tm)
        passes.append((jnp.minimum(starts[g] // MOE_BLOCK, nblk - 1), jnp.int32(g), inside.astype(F32)))

    def gate_up(ps):
        b, g, _ = ps
        xb = xs_ref[pl.ds(pl.multiple_of(b * MOE_BLOCK, MOE_BLOCK), MOE_BLOCK), :]
        return [jnp.dot(xb, wgu_ref[g * EXP_PER_GROUP + j], preferred_element_type=F32)
                for j in range(EXP_PER_GROUP)]

    def finish(ps, gus):
        b, g, live = ps
        rows = pl.ds(pl.multiple_of(b * MOE_BLOCK, MOE_BLOCK), MOE_BLOCK)
        cb = cs_ref[rows, :] * live
        lane_b = lax.broadcasted_iota(jnp.int32, cb.shape, 1)
        acts = []
        for j in range(EXP_PER_GROUP):
            ce = jnp.sum(jnp.where(lane_b == EXPERT_LANE0 + g * EXP_PER_GROUP + j, cb, 0.0), axis=1,
                         keepdims=True)
            gt = gus[j][:, 0:D_EXPERT]
            acts.append((gt * jax.nn.sigmoid(gt) * gus[j][:, D_EXPERT:2 * D_EXPERT] * ce).astype(BF16))
        wrows = pl.ds(pl.multiple_of(g * (EXP_PER_GROUP * D_EXPERT), EXP_PER_GROUP * D_EXPERT),
                      EXP_PER_GROUP * D_EXPERT)
        y = jnp.dot(jnp.concatenate(acts, axis=1), wd_ref[wrows, :], preferred_element_type=F32)
        ys_ref[rows, :] = ys_ref[rows, :] + y

    pending = [gate_up(passes[0])]
    for i, ps in enumerate(passes):
        gus = pending.pop(0)
        if i + 1 < len(passes):
            pending.append(gate_up(passes[i + 1]))
        finish(ps, gus)

    ys = ys_ref[...]
    y_hi = ys.astype(BF16)
    y_lo = (ys - y_hi.astype(F32)).astype(BF16)
    y = x1_ref[...] + (jnp.dot(perm_t, y_hi, preferred_element_type=F32)
                       + jnp.dot(perm_t, y_lo, preferred_element_type=F32))
    if final_norm:
        y = _rms(y, gf_ref[...])
    o_ref[...] = y


def _moe_call(x1, h2, lg, lw, final_g, final_norm):
    t, d = x1.shape
    tm = TM_MOE
    tok = lambda w: pl.BlockSpec((tm, w), lambda i: (i, 0))
    once = pl.Buffered(1)
    idx = jnp.arange(tm)
    ltri = (idx[None, :] < idx[:, None]).astype(BF16)
    return pl.pallas_call(
        functools.partial(_moe_kernel, final_norm=final_norm), grid=(t // tm,),
        in_specs=[tok(d), tok(d), tok(ROUTER_COLS),
                  pl.BlockSpec((tm, tm), lambda i: (0, 0), pipeline_mode=once),
                  pl.BlockSpec((N_EXPERTS, d, 2 * D_EXPERT), lambda i: (0, 0, 0), pipeline_mode=once),
                  pl.BlockSpec((N_EXPERTS * D_EXPERT, d), lambda i: (0, 0), pipeline_mode=once),
                  pl.BlockSpec((1, d), lambda i: (0, 0))],
        out_specs=tok(d),
        out_shape=jax.ShapeDtypeStruct((t, d), F32),
        scratch_shapes=[pltpu.VMEM((tm, d), BF16), pltpu.VMEM((tm, ROUTER_COLS), F32),
                        pltpu.VMEM((tm, d), F32)],
        compiler_params=pltpu.CompilerParams(dimension_semantics=("parallel",),
                                             vmem_limit_bytes=VMEM_LIMIT),
        name="moe",
    )(x1, h2, lg, ltri, lw["w_gu"], lw["w_dn"], final_g)


def _layer_weights(l, w_in, ln1_g, sink_a, q_norm_g, w_uq, kv_norm_g, w_ukv, q_norm_c, k_norm_c,
                   w_br_a, w_br_b, w_br_c, w_out, ln2_g, w_grp, b_grp, w_exr, b_exr, w_gate, w_up, w_down):
    wi = w_in[l]
    d = wi.shape[0]
    z = lambda n: jnp.zeros((d, n), F32)
    w_att = jnp.concatenate([
        wi[:, 0:1152], z(64), wi[:, 1152:1184], z(32), wi[:, 1184:1696],
        wi[:, 1696:1760], z(64), wi[:, 1760:1824], z(64), wi[:, 1824:1952]], axis=1).astype(BF16)
    wq = w_uq[l].reshape(Q_LORA, HB, NOPE_B + ROPE_B)
    wq = jnp.concatenate([wq, jnp.zeros((Q_LORA, HB, HEAD_PAD - NOPE_B - ROPE_B), F32)], axis=-1)
    wkv = w_ukv[l].reshape(KV_LORA, HB, NOPE_B + V_B)
    wk = jnp.concatenate([wkv[..., :NOPE_B], jnp.zeros((KV_LORA, HB, HEAD_PAD - NOPE_B), F32)], axis=-1)
    zero64 = jnp.zeros((HD_C,), F32)
    w_router = jnp.concatenate([w_grp[l], w_exr[l], jnp.zeros((d, ROUTER_COLS - N_GROUPS - N_EXPERTS), F32)], 1)
    b_router = jnp.concatenate([b_grp[l], b_exr[l], jnp.zeros((ROUTER_COLS - N_GROUPS - N_EXPERTS,), F32)])
    return dict(
        ln1_g=ln1_g[l].reshape(1, d), w_att=w_att, w_gate3=wi[:, 1952:].astype(BF16),
        sink=jnp.repeat(sink_a[l] * LOG2E, TQ_WINDOW).reshape(HA_Q // 2, 1, 2 * TQ_WINDOW), q_norm_g=q_norm_g[l].reshape(1, Q_LORA),
        w_uq=wq.reshape(Q_LORA, HB * HEAD_PAD).astype(BF16),
        kv_norm_g=kv_norm_g[l].reshape(1, KV_LORA),
        w_uk=wk.reshape(KV_LORA, HB * HEAD_PAD).astype(BF16),
        w_uv=wkv[..., NOPE_B:].reshape(KV_LORA, HB * V_B).astype(BF16),
        gqc=jnp.tile(q_norm_c[l], HC_Q).reshape(1, 512),
        gkc=jnp.concatenate([k_norm_c[l], zero64, k_norm_c[l], zero64]).reshape(1, 256),
        w_br_a=w_br_a[l].astype(BF16), w_br_b=w_br_b[l].astype(BF16), w_br_c=w_br_c[l].astype(BF16),
        w_out=w_out[l].astype(BF16), ln2_g=ln2_g[l].reshape(1, d),
        w_router=w_router.astype(BF16), b_router=b_router.reshape(1, ROUTER_COLS),
        w_gu=jnp.concatenate([w_gate[l], w_up[l]], axis=-1).astype(BF16),
        w_dn=w_down[l].reshape(N_EXPERTS * D_EXPERT, d).astype(BF16),
    )


def _t5_bucket(rel):
    nb = N_BUCKETS // 2
    bucket = jnp.where(rel > 0, nb, 0)
    n = jnp.abs(rel)
    max_exact = nb // 2
    nf = jnp.maximum(n, 1).astype(F32)
    large = max_exact + (jnp.log(nf / max_exact) / math.log(MAX_DISTANCE / max_exact)
                         * (nb - max_exact)).astype(jnp.int32)
    large = jnp.minimum(large, nb - 1)
    return bucket + jnp.where(n < max_exact, n, large)


def _tables(s, rel_bias):
    half = ROPE_B // 2
    inv = ROPE_THETA ** (-jnp.arange(half, dtype=F32) / half)
    pos = jnp.arange(s, dtype=jnp.int32)
    row = (pos // GRID_W).astype(F32)
    col = (pos % GRID_W).astype(F32)

    def cs(p):
        ang = p[:, None] * inv[None, :]
        return jnp.cos(ang), jnp.sin(ang)

    cp, sp = cs(pos.astype(F32))
    cr, sr = cs(row)
    cc, sc = cs(col)
    one = jnp.ones((s, 64), F32)
    z32 = jnp.zeros((s, 32), F32)
    z64 = jnp.zeros((s, 64), F32)
    b_cos = jnp.concatenate([one, cp, cp, z32], axis=1)
    b_sin = jnp.concatenate([z64, -sp, sp, z32], axis=1)
    c_cos = jnp.concatenate([cr, cr, cc, cc], axis=1)
    c_sin = jnp.concatenate([-sr, sr, -sc, sc], axis=1)
    scale_b = (NOPE_B + ROPE_B) ** -0.5 * LOG2E
    scale_c = HD_C ** -0.5 * LOG2E
    gidx = jnp.arange(512) // HD_C
    gmat = (gidx[:, None] == gidx[None, :]).astype(BF16)
    period = 4 * TQ_WINDOW + 1
    mats = []
    for shift in (0, WINDOW, 2 * WINDOW):
        rel = jnp.arange(period) - TQ_WINDOW - shift
        prof = jnp.where((jnp.abs(rel) <= WINDOW)[:, None], rel_bias[_t5_bucket(rel)].astype(F32) * LOG2E,
                         NEG_INF).T
        skew = jnp.tile(prof, (1, TQ_WINDOW))[:, :TQ_WINDOW * (period - 1)]
        skew = skew.reshape(HA_Q, TQ_WINDOW, period - 1)[:, :, TQ_WINDOW:TQ_WINDOW + TK_WINDOW]
        bias = jnp.transpose(skew, (0, 2, 1)).reshape(HA_Q // 2, 2, TK_WINDOW, TQ_WINDOW)
        mats.append(jnp.transpose(bias, (0, 2, 1, 3)).reshape(HA_Q // 2, TK_WINDOW, 2 * TQ_WINDOW))
    return dict(
        bqc=b_cos * scale_b, bqs=b_sin * scale_b, bkc=b_cos, bks=b_sin,
        cqc=jnp.concatenate([c_cos, c_cos], 1) * scale_c, cqs=jnp.concatenate([c_sin, c_sin], 1) * scale_c,
        ckc=jnp.concatenate([c_cos, z64], 1), cks=jnp.concatenate([c_sin, z64], 1),
        gmat=gmat, bias3=jnp.stack(mats, axis=0), **_key_norm_constants())


def _key_norm_constants():
    lb = jnp.arange(HB * HEAD_PAD)
    ghb = ((lb[:, None] // HEAD_PAD == jnp.arange(LANE)[None, :]) & (lb[:, None] % HEAD_PAD < ONES_LANE_B))
    lc = jnp.arange(HC_KV * LANE)
    ghc = ((lc[:, None] // LANE == jnp.arange(LANE)[None, :]) & (lc[:, None] % LANE < ONES_LANE_C))
    return dict(ghb=ghb.astype(BF16), ghc=ghc.astype(BF16),
                oneb=(lb % HEAD_PAD == ONES_LANE_B).astype(F32).reshape(1, -1),
                onec=(lc % LANE == ONES_LANE_C).astype(F32).reshape(1, -1))


def _trunk(x, layers, tabs, final_g):
    b, s, d = x.shape
    n_layers = len(layers)
    for l, lw in enumerate(layers):
        qa, ka, va, qbt, kb, vbt, qct, kc, vct, kbn, kcn = _prep_call(x, lw, tabs)
        oa = _window_call(qa, ka, va, lw["sink"], tabs["bias3"])
        ob = _dense_call(qbt, kb, vbt, kbn, n_heads=HB, heads_per_kv=1, dq=HEAD_PAD, dk=HEAD_PAD,
                         k_stride=HEAD_PAD, ones_lane=ONES_LANE_B, name="latent_attn")
        oc = _dense_call(qct, kc, vct, kcn, n_heads=HC_Q, heads_per_kv=HC_Q // HC_KV, dq=HD_C, dk=HD_C,
                         k_stride=LANE, ones_lane=ONES_LANE_C, name="axial_attn")
        t = b * s
        x1, h2, lg = _merge_call(x.reshape(t, d), oa.reshape(t, 512), ob.reshape(t, 512),
                                 oc.reshape(t, 512), lw)
        x = _moe_call(x1, h2, lg, lw, final_g, l == n_layers - 1).reshape(b, s, d)
    return x


def kernel(x_prompt, x_sample, rel_bias, final_g, ln1_g, w_in, sink_a, q_norm_g, w_uq, kv_norm_g, w_ukv,
           q_norm_c, k_norm_c, w_br_a, w_br_b, w_br_c, w_out, ln2_g, w_grp, b_grp, w_exr, b_exr, w_gate,
           w_up, w_down):
    n_layers = w_in.shape[0]
    layers = [_layer_weights(l, w_in, ln1_g, sink_a, q_norm_g, w_uq, kv_norm_g, w_ukv, q_norm_c, k_norm_c,
                             w_br_a, w_br_b, w_br_c, w_out, ln2_g, w_grp, b_grp, w_exr, b_exr, w_gate,
                             w_up, w_down) for l in range(n_layers)]
    fg = final_g.reshape(1, -1)
    outs = []
    for x in (x_prompt, x_sample):
        tabs = _tables(x.shape[1], rel_bias)
        outs.append(_trunk(x, layers, tabs, fg))
    return tuple(outs)
```

```python
import functools
import math

import jax
import jax.numpy as jnp
from jax import lax
from jax.experimental import pallas as pl
from jax.experimental.pallas import tpu as pltpu

F32 = jnp.float32
BF16 = jnp.bfloat16

D_MODEL = 1024
GRID_W = 64
BLOCK = 128
EPS = 1e-6
NEG_INF = -1e30
ROPE_THETA = 10000.0
HA_Q, HA_KV, HD_A = 8, 2, 64
WINDOW = 128
N_BUCKETS = 32
MAX_DISTANCE = 128
HB, Q_LORA, KV_LORA, NOPE_B, ROPE_B, V_B = 8, 256, 128, 64, 32, 64
HC_Q, HC_KV, HD_C = 8, 2, 64
N_BRANCHES = 3
N_GROUPS, EXP_PER_GROUP = 4, 4
N_EXPERTS = N_GROUPS * EXP_PER_GROUP
D_EXPERT = 256

LANE = 128
HEAD_PAD = 128
PREP_ROWS_T = 1664
PREP_COLS_K = 640
ROUTER_COLS = 128
GROUP_LANE0 = 0
EXPERT_LANE0 = N_GROUPS

TM_PREP = 256
TK_DENSE = TM_PREP
TQ_DENSE = 1024
TQ_WINDOW = 256
TK_WINDOW = TQ_WINDOW + 2 * WINDOW
WINDOW_SUBTILES = 2
CHUNKS_PER_TRIP = 1
SCORE_LOOKAHEAD = 2
SCORE_SLOTS = 4
ACC_ROWS = 80
LOG2E = math.log2(math.e)
ONES_LANE_B = NOPE_B + ROPE_B
ONES_LANE_C = HD_C
BOUND_SLACK = 1.02
SHIFT_BOUND_MAX = 48.0
TM_MERGE = 256
TM_MOE = 512
MOE_BLOCK = 128
VMEM_LIMIT = 56 * 1024 * 1024


def _rms(x, g):
    return x * lax.rsqrt(jnp.mean(x * x, axis=-1, keepdims=True) + EPS) * g


def _swap16(x):
    n = x.shape[-1]
    lane = lax.broadcasted_iota(jnp.int32, x.shape, x.ndim - 1)
    return jnp.where((lane & 16) == 0, pltpu.roll(x, n - 16, x.ndim - 1), pltpu.roll(x, 16, x.ndim - 1))


def _tile_lanes(t, reps):
    return jnp.concatenate([t] * reps, axis=-1) if reps > 1 else t


def _group_ssq(x, g_ref):
    sq = x * x
    hi = sq.astype(BF16)
    lo = (sq - hi.astype(F32)).astype(BF16)
    g = g_ref[...]
    return (jnp.dot(hi, g, preferred_element_type=F32) + jnp.dot(lo, g, preferred_element_type=F32))


def _max_sq_norm(k, g_ref):
    kr = k.astype(BF16).astype(F32)
    sq = jnp.dot((kr * kr).astype(BF16), g_ref[...], preferred_element_type=F32)
    return jnp.max(sq, axis=0, keepdims=True)


def _rms_rows(x, g_col):
    return x * lax.rsqrt(jnp.mean(x * x, axis=0, keepdims=True) + EPS) * g_col


def _swap16_rows(x, lo, hi):
    pieces = []
    for base in range(0, x.shape[0], HEAD_PAD):
        pieces.append(x[base:base + lo])
        for r in range(base + lo, base + hi, 32):
            pieces += [x[r + 16:r + 32], x[r:r + 16]]
        pieces.append(x[base + hi:base + HEAD_PAD])
    return jnp.concatenate([p for p in pieces if p.shape[0]], axis=0)


def _prep_kernel(x_ref, g1_ref, wqt_ref, wk_ref, qngc_ref, wuqt_ref, kvngc_ref, kvng_ref, wuk_ref, wuvt_ref,
                 gqcc_ref, gkc_ref, gmat_ref, ghb_ref, ghc_ref, oneb_ref, onec_ref,
                 bqct_ref, bqst_ref, bkc_ref, bks_ref, cqct_ref, cqst_ref, ckc_ref, cks_ref,
                 qa_ref, ka_ref, va_ref, qbt_ref, kb_ref, vbt_ref, qct_ref, kc_ref, vct_ref,
                 kbn_ref, kcn_ref):
    x = x_ref[0]
    h = _rms(x, g1_ref[...])
    hb = h.astype(BF16)
    ht = h.T.astype(BF16)

    lat = jnp.dot(wqt_ref[0:384, :], ht, preferred_element_type=F32)
    pk = jnp.dot(hb, wk_ref[...], preferred_element_type=F32)
    qct_raw = jnp.dot(wqt_ref[384:896, :], ht, preferred_element_type=F32)
    cqt = _rms_rows(lat[0:256], qngc_ref[...]).astype(BF16)
    ckvt = _rms_rows(lat[256:384], kvngc_ref[...]).astype(BF16)
    ckv = _rms(pk[:, 128:256], kvng_ref[...]).astype(BF16)
    qbt = jnp.dot(wuqt_ref[...], cqt, preferred_element_type=F32)
    kb = jnp.dot(ckv, wuk_ref[...], preferred_element_type=F32)
    vbt = jnp.dot(wuvt_ref[...], ckvt, preferred_element_type=F32)
    pt = jnp.dot(wqt_ref[896:PREP_ROWS_T, :], ht, preferred_element_type=F32)
    kc = pk[:, 384:640]
    ssq_k = _group_ssq(kc, gmat_ref)

    qa_ref[0] = (pt[0:512] * (HD_A ** -0.5 * LOG2E)).astype(BF16)
    ka_ref[0] = pk[:, 0:128].astype(BF16)
    vat = pt[512:640].astype(BF16)
    for r in range(TM_PREP // BLOCK):
        va_ref[0, r] = vat[:, r * BLOCK:(r + 1) * BLOCK]

    rep = lambda t, n: jnp.concatenate([t] * n, axis=0)
    qbt = qbt * rep(bqct_ref[...], HB) + _swap16_rows(qbt, NOPE_B, NOPE_B + ROPE_B) * rep(bqst_ref[...], HB)
    qbt_ref[0] = qbt.astype(BF16)
    kpe = pk[:, 256:384]
    kpe = kpe * bkc_ref[...] + _swap16(kpe) * bks_ref[...]
    kb = kb + _tile_lanes(kpe, HB)
    kbn_ref[0, 0] = _max_sq_norm(kb, ghb_ref)
    kb_ref[0] = (kb + oneb_ref[...]).astype(BF16)
    vbt_ref[0, 0] = vbt.astype(BF16)

    heads = []
    for hh in range(HC_Q):
        blk = qct_raw[hh * HD_C:(hh + 1) * HD_C]
        heads.append(blk * lax.rsqrt(jnp.mean(blk * blk, axis=0, keepdims=True) + EPS))
    qct = jnp.concatenate(heads, axis=0) * gqcc_ref[...]
    qct = qct * rep(cqct_ref[...], HC_Q // 2) + _swap16_rows(qct, 0, HEAD_PAD) * rep(cqst_ref[...], HC_Q // 2)
    qct_ref[0] = qct.astype(BF16)
    kc = kc * lax.rsqrt(ssq_k * (1.0 / HD_C) + EPS) * gkc_ref[...]
    kc = kc * _tile_lanes(ckc_ref[...], 2) + _swap16(kc) * _tile_lanes(cks_ref[...], 2)
    kcn_ref[0, 0] = _max_sq_norm(kc, ghc_ref)
    kc_ref[0] = (kc + onec_ref[...]).astype(BF16)
    vct_ref[0, 0] = pt[640:768].astype(BF16)


def _prep_call(x, lw, tabs):
    b, s, d = x.shape
    tm = TM_PREP
    nt = s // tm
    const = lambda *shape: pl.BlockSpec(shape, lambda bi, ti: (0,) * len(shape))
    tab = lambda w: pl.BlockSpec((tm, w), lambda bi, ti: (ti, 0))
    tabt = pl.BlockSpec((LANE, tm), lambda bi, ti: (0, ti))
    in_specs = [
        pl.BlockSpec((1, tm, d), lambda bi, ti: (bi, ti, 0)),
        const(1, d), const(PREP_ROWS_T, d), const(d, PREP_COLS_K), const(Q_LORA, 1),
        const(HB * HEAD_PAD, Q_LORA), const(KV_LORA, 1), const(1, KV_LORA), const(KV_LORA, HB * HEAD_PAD),
        const(HB * V_B, KV_LORA),
        const(512, 1), const(1, 256), const(256, 256),
        const(HB * HEAD_PAD, LANE), const(256, LANE), const(1, HB * HEAD_PAD), const(1, 256),
        tabt, tabt, tab(LANE), tab(LANE), tabt, tabt, tab(LANE), tab(LANE),
    ]
    tok = lambda w: pl.BlockSpec((1, tm, w), lambda bi, ti: (bi, ti, 0))
    tr = lambda w: pl.BlockSpec((1, w, tm), lambda bi, ti: (bi, 0, ti))
    chunked = lambda w: pl.BlockSpec((1, 1, w, tm), lambda bi, ti: (bi, ti, 0, 0))
    blocks = pl.BlockSpec((1, tm // BLOCK, 128, BLOCK), lambda bi, ti: (bi, ti, 0, 0))
    tile_row = pl.BlockSpec((1, 1, 1, LANE), lambda bi, ti: (bi, ti, 0, 0))
    out_specs = [tr(512), tok(128), blocks, tr(HB * HEAD_PAD), tok(HB * HEAD_PAD), chunked(HB * V_B),
                 tr(512), tok(256), chunked(128), tile_row, tile_row]
    out_shape = [
        jax.ShapeDtypeStruct((b, 512, s), BF16), jax.ShapeDtypeStruct((b, s, 128), BF16),
        jax.ShapeDtypeStruct((b, s // BLOCK, 128, BLOCK), BF16),
        jax.ShapeDtypeStruct((b, HB * HEAD_PAD, s), BF16), jax.ShapeDtypeStruct((b, s, HB * HEAD_PAD), BF16),
        jax.ShapeDtypeStruct((b, nt, HB * V_B, tm), BF16),
        jax.ShapeDtypeStruct((b, 512, s), BF16), jax.ShapeDtypeStruct((b, s, 256), BF16),
        jax.ShapeDtypeStruct((b, nt, 128, tm), BF16),
        jax.ShapeDtypeStruct((b, nt, 1, LANE), F32), jax.ShapeDtypeStruct((b, nt, 1, LANE), F32),
    ]
    return pl.pallas_call(
        _prep_kernel, grid=(b, nt), in_specs=in_specs, out_specs=out_specs, out_shape=out_shape,
        compiler_params=pltpu.CompilerParams(dimension_semantics=("parallel", "parallel"),
                                             vmem_limit_bytes=VMEM_LIMIT),
        name="prep",
    )(x, lw["ln1_g"], lw["w_qt"], lw["w_k"], lw["q_norm_col"], lw["w_uqt"], lw["kv_norm_col"], lw["kv_norm_g"],
      lw["w_uk"], lw["w_uvt"], lw["gqc_col"], lw["gkc"], tabs["gmat"], tabs["ghb"], tabs["ghc"], tabs["oneb"],
      tabs["onec"], tabs["bqct"], tabs["bqst"], tabs["bkc"], tabs["bks"], tabs["cqct"], tabs["cqst"],
      tabs["ckc"], tabs["cks"])


def _window_kernel(sink_ref, qt_ref, k_ref, vt_ref, bias0_ref, bias1_ref, o_ref, ot_ref, st_ref):
    nblk = vt_ref.shape[1]
    wblk = TK_WINDOW // BLOCK
    grp = HA_Q // HA_KV
    npair = HA_Q // 2
    bias_refs = (bias0_ref, bias1_ref)
    ones_rows = (lax.broadcasted_iota(jnp.int32, (ACC_ROWS - HD_A, TK_WINDOW), 0) == 0).astype(BF16)

    kgs, vexts = [], []
    for sub in range(WINDOW_SUBTILES):
        i = pl.program_id(1) * WINDOW_SUBTILES + sub
        cs = jnp.clip(i * (TQ_WINDOW // BLOCK) - 1, 0, nblk - wblk)
        kwin = k_ref[0, pl.ds(pl.multiple_of(cs * BLOCK, BLOCK), TK_WINDOW), :]
        for g in range(HA_KV):
            kgs.append(kwin[:, g * HD_A:(g + 1) * HD_A])
            vwin = jnp.concatenate([vt_ref[0, cs + r, g * HD_A:(g + 1) * HD_A, :] for r in range(wblk)],
                                   axis=1)
            vexts.append(jnp.concatenate([vwin, ones_rows], axis=0))

    def scores(n):
        sub, c = divmod(n, npair)
        cols = slice(sub * TQ_WINDOW, (sub + 1) * TQ_WINDOW)
        qt = jnp.concatenate([qt_ref[0, h * HD_A:(h + 1) * HD_A, cols] for h in (2 * c, 2 * c + 1)], axis=1)
        return jnp.dot(kgs[sub * HA_KV + 2 * c // grp], qt, preferred_element_type=F32)

    nchain = WINDOW_SUBTILES * npair
    for n in range(SCORE_LOOKAHEAD):
        st_ref[n] = scores(n)
    for n in range(nchain):
        sub, c = divmod(n, npair)
        if n + SCORE_LOOKAHEAD < nchain:
            st_ref[(n + SCORE_LOOKAHEAD) % SCORE_SLOTS] = scores(n + SCORE_LOOKAHEAD)
        st = st_ref[n % SCORE_SLOTS] + bias_refs[sub][0, c]
        sink = sink_ref[c]
        m = jnp.maximum(jnp.max(st, axis=0, keepdims=True), sink)
        p = jnp.exp2(st - m).astype(BF16)
        pv = jnp.dot(vexts[sub * HA_KV + 2 * c // grp], p, preferred_element_type=F32)
        o2 = pv[0:HD_A, :] / (pv[HD_A:HD_A + 1, :] + jnp.exp2(sink - m))
        for r in range(2):
            h = 2 * c + r
            ot_ref[h * HD_A:(h + 1) * HD_A, sub * TQ_WINDOW:(sub + 1) * TQ_WINDOW] = (
                o2[:, r * TQ_WINDOW:(r + 1) * TQ_WINDOW])
    o_ref[0] = ot_ref[...].T.astype(BF16)


def _window_call(qat, ka, vat, sink, bias3):
    b, _, s = qat.shape
    nq = s // TQ_WINDOW
    tstep = WINDOW_SUBTILES * TQ_WINDOW

    def variant(sub):
        def index(bi, i):
            t = i * WINDOW_SUBTILES + sub
            return (jnp.where(t == 0, 0, jnp.where(t == nq - 1, 2, 1)), 0, 0, 0)
        return pl.BlockSpec((1, HA_Q // 2, TK_WINDOW, 2 * TQ_WINDOW), index)

    return pl.pallas_call(
        _window_kernel, grid=(b, s // tstep),
        in_specs=[
            pl.BlockSpec((HA_Q // 2, 1, 2 * TQ_WINDOW), lambda bi, i: (0, 0, 0)),
            pl.BlockSpec((1, 512, tstep), lambda bi, i: (bi, 0, i)),
            pl.BlockSpec((1, s, 128), lambda bi, i: (bi, 0, 0)),
            pl.BlockSpec((1, s // BLOCK, 128, BLOCK), lambda bi, i: (bi, 0, 0, 0)),
            variant(0), variant(1),
        ],
        out_specs=pl.BlockSpec((1, tstep, 512), lambda bi, i: (bi, i, 0)),
        out_shape=jax.ShapeDtypeStruct((b, s, 512), BF16),
        scratch_shapes=[pltpu.VMEM((HA_Q * HD_A, tstep), F32),
                        pltpu.VMEM((SCORE_SLOTS, TK_WINDOW, 2 * TQ_WINDOW), F32)],
        compiler_params=pltpu.CompilerParams(dimension_semantics=("parallel", "arbitrary"),
                                             vmem_limit_bytes=VMEM_LIMIT),
        name="window_attn",
    )(sink, qat, ka, vat, bias3, bias3)


def _dense_kernel(qt_ref, k_ref, vt_ref, kn_ref, o_ref, acc_ref, ot_ref, st_ref, qs_ref, *, n_heads,
                  heads_per_kv, dq, dk, k_stride, ones_lane):
    nchunk = vt_ref.shape[1]
    tkc = vt_ref.shape[3]
    tq = qt_ref.shape[2]
    tk = CHUNKS_PER_TRIP * tkc
    ntrip = nchunk // CHUNKS_PER_TRIP
    ones_rows = (lax.broadcasted_iota(jnp.int32, (ACC_ROWS - 64, tkc), 0) == 0).astype(BF16)

    kn2 = jnp.max(kn_ref[0, :, 0, :], axis=0, keepdims=True)
    lane = lax.broadcasted_iota(jnp.int32, kn2.shape, 1)
    row = lax.broadcasted_iota(jnp.int32, (HEAD_PAD, tq), 0)
    bound = None
    for h in range(n_heads):
        g = h // heads_per_kv
        q = qt_ref[0, h * dq:h * dq + dk, :]
        if dk < HEAD_PAD:
            q = jnp.concatenate([q, jnp.zeros((HEAD_PAD - dk, tq), BF16)], axis=0)
        qf = q.astype(F32)
        kmax2 = jnp.sum(jnp.where(lane == g, kn2, 0.0), axis=1, keepdims=True)
        u = jnp.sqrt(jnp.sum(qf * qf, axis=0, keepdims=True) * kmax2) * BOUND_SLACK
        qs_ref[h * HEAD_PAD:(h + 1) * HEAD_PAD, :] = jnp.where(row == ones_lane, -u, qf).astype(BF16)
        hmax = jnp.max(u)
        bound = hmax if bound is None else jnp.maximum(bound, hmax)
    shifted_ok = bound <= SHIFT_BOUND_MAX

    def scores(t, h, shifted):
        g = h // heads_per_kv
        off = pl.multiple_of(t * tk, tk)
        if shifted:
            qt = qs_ref[h * HEAD_PAD:(h + 1) * HEAD_PAD, :]
            kc = k_ref[0, pl.ds(off, tk), g * k_stride:g * k_stride + HEAD_PAD]
        else:
            qt = qt_ref[0, h * dq:h * dq + dk, :]
            kc = k_ref[0, pl.ds(off, tk), g * k_stride:g * k_stride + dk]
        return jnp.dot(kc, qt, preferred_element_type=F32)

    def attend(shifted):
        acc_ref[...] = jnp.zeros(acc_ref.shape, F32)
        for j in range(SCORE_LOOKAHEAD):
            st_ref[j] = scores(0, j, shifted)

        def body(t, ms):
            t_next = jnp.minimum(t + 1, ntrip - 1)
            new_ms = []
            for h in range(n_heads):
                g = h // heads_per_kv
                ahead = h + SCORE_LOOKAHEAD
                if ahead < n_heads:
                    st_ref[ahead % SCORE_SLOTS] = scores(t, ahead, shifted)
                else:
                    st_ref[ahead % SCORE_SLOTS] = scores(t_next, ahead - n_heads, shifted)
                st = st_ref[h % SCORE_SLOTS]
                if shifted:
                    p = jnp.exp2(st).astype(BF16)
                else:
                    m_new = jnp.maximum(ms[h], jnp.max(st, axis=0, keepdims=True))
                    alpha = jnp.exp2(ms[h] - m_new)
                    p = jnp.exp2(st - m_new).astype(BF16)
                    new_ms.append(m_new)
                pv = None
                for cc in range(CHUNKS_PER_TRIP):
                    vc = vt_ref[0, t * CHUNKS_PER_TRIP + cc, g * 64:(g + 1) * 64, :]
                    part = jnp.dot(jnp.concatenate([vc, ones_rows], axis=0), p[cc * tkc:(cc + 1) * tkc, :],
                                   preferred_element_type=F32)
                    pv = part if pv is None else pv + part
                rows = slice(h * ACC_ROWS, (h + 1) * ACC_ROWS)
                if shifted:
                    acc_ref[rows, :] = acc_ref[rows, :] + pv
                else:
                    acc_ref[rows, :] = alpha * acc_ref[rows, :] + pv
            return tuple(new_ms)

        init = () if shifted else (jnp.full((1, tq), -jnp.inf, F32),) * n_heads
        lax.fori_loop(0, ntrip, body, init)

    pl.when(shifted_ok)(functools.partial(attend, True))
    pl.when(jnp.logical_not(shifted_ok))(functools.partial(attend, False))

    for h in range(n_heads):
        num = acc_ref[h * ACC_ROWS:h * ACC_ROWS + 64, :]
        den = acc_ref[h * ACC_ROWS + 64:h * ACC_ROWS + 65, :]
        ot_ref[h * 64:(h + 1) * 64, :] = num * (1.0 / den)
    o_ref[0] = ot_ref[...].T.astype(BF16)


def _dense_call(qt, k, vt, kn, *, n_heads, heads_per_kv, dq, dk, k_stride, ones_lane, name):
    b, fq, s = qt.shape
    tq = TQ_DENSE
    fk = k.shape[2]
    _, nchunk, fv, tk = vt.shape
    nt = kn.shape[1]
    kern = functools.partial(_dense_kernel, n_heads=n_heads, heads_per_kv=heads_per_kv, dq=dq, dk=dk,
                             k_stride=k_stride, ones_lane=ones_lane)
    return pl.pallas_call(
        kern, grid=(b, s // tq),
        in_specs=[
            pl.BlockSpec((1, fq, tq), lambda bi, i: (bi, 0, i)),
            pl.BlockSpec((1, s, fk), lambda bi, i: (bi, 0, 0)),
            pl.BlockSpec((1, nchunk, fv, tk), lambda bi, i: (bi, 0, 0, 0)),
            pl.BlockSpec((1, nt, 1, LANE), lambda bi, i: (bi, 0, 0, 0)),
        ],
        out_specs=pl.BlockSpec((1, tq, n_heads * 64), lambda bi, i: (bi, i, 0)),
        out_shape=jax.ShapeDtypeStruct((b, s, n_heads * 64), BF16),
        scratch_shapes=[pltpu.VMEM((n_heads * ACC_ROWS, tq), F32), pltpu.VMEM((n_heads * 64, tq), F32),
                        pltpu.VMEM((SCORE_SLOTS, CHUNKS_PER_TRIP * tk, tq), F32),
                        pltpu.VMEM((n_heads * HEAD_PAD, tq), BF16)],
        compiler_params=pltpu.CompilerParams(dimension_semantics=("parallel", "arbitrary"),
                                             vmem_limit_bytes=VMEM_LIMIT),
        name=name,
    )(qt, k, vt, kn)


def _merge_kernel(x_ref, oa_ref, ob_ref, oc_ref, g1_ref, wg_ref, wa_ref, wb_ref, wc_ref, wo_ref,
                  g2_ref, wr_ref, br_ref, x1_ref, h2_ref, lg_ref):
    x = x_ref[...]
    ys = [jnp.dot(o_ref[...], w_ref[...], preferred_element_type=F32)
          for o_ref, w_ref in ((oa_ref, wa_ref), (ob_ref, wb_ref), (oc_ref, wc_ref))]
    hb = _rms(x, g1_ref[...]).astype(BF16)
    merged = None
    for j in range(N_BRANCHES):
        gl = jnp.dot(hb, wg_ref[:, j * D_MODEL:(j + 1) * D_MODEL], preferred_element_type=F32)
        t = jax.nn.sigmoid(gl) * ys[j]
        merged = t if merged is None else merged + t
    x1 = x + jnp.dot(merged.astype(BF16), wo_ref[...], preferred_element_type=F32)
    x1_ref[...] = x1
    h2 = _rms(x1, g2_ref[...])
    h2b = h2.astype(BF16)
    h2_ref[...] = h2b
    lg_ref[...] = jnp.dot(h2b, wr_ref[...], preferred_element_type=F32) + br_ref[...]


def _merge_call(x2d, oa, ob, oc, lw):
    t, d = x2d.shape
    tm = TM_MERGE
    const = lambda *shape: pl.BlockSpec(shape, lambda i: (0,) * len(shape))
    tok = lambda w: pl.BlockSpec((tm, w), lambda i: (i, 0))
    return pl.pallas_call(
        _merge_kernel, grid=(t // tm,),
        in_specs=[tok(d), tok(512), tok(512), tok(512), const(1, d), const(d, N_BRANCHES * d),
                  const(512, d), const(512, d), const(512, d), const(d, d), const(1, d),
                  const(d, ROUTER_COLS), const(1, ROUTER_COLS)],
        out_specs=[tok(d), tok(d), tok(ROUTER_COLS)],
        out_shape=[jax.ShapeDtypeStruct((t, d), F32), jax.ShapeDtypeStruct((t, d), BF16),
                   jax.ShapeDtypeStruct((t, ROUTER_COLS), F32)],
        compiler_params=pltpu.CompilerParams(dimension_semantics=("parallel",),
                                             vmem_limit_bytes=VMEM_LIMIT),
        name="merge",
    )(x2d, oa, ob, oc, lw["ln1_g"], lw["w_gate3"], lw["w_br_a"], lw["w_br_b"], lw["w_br_c"], lw["w_out"],
      lw["ln2_g"], lw["w_router"], lw["b_router"])


def _combine_weights(lg):
    lane = lax.broadcasted_iota(jnp.int32, lg.shape, 1)
    big = jnp.int32(1 << 20)
    ninf = jnp.float32(-jnp.inf)
    is_g = lane < N_GROUPS
    gl = jnp.where(is_g, lg, ninf)
    gmax = jnp.max(gl, axis=-1, keepdims=True)
    grp = jnp.min(jnp.where(is_g & (lg == gmax), lane, big), axis=-1, keepdims=True)
    gsum = jnp.sum(jnp.where(is_g, jnp.exp(gl - gmax), 0.0), axis=-1, keepdims=True)
    grp_w = 1.0 / gsum
    lo = EXPERT_LANE0 + grp * EXP_PER_GROUP
    in_grp = (lane >= lo) & (lane < lo + EXP_PER_GROUP)
    v1 = jnp.max(jnp.where(in_grp, lg, ninf), axis=-1, keepdims=True)
    i1 = jnp.min(jnp.where(in_grp & (lg == v1), lane, big), axis=-1, keepdims=True)
    rest = in_grp & (lane != i1)
    v2 = jnp.max(jnp.where(rest, lg, ninf), axis=-1, keepdims=True)
    i2 = jnp.min(jnp.where(rest & (lg == v2), lane, big), axis=-1, keepdims=True)
    e21 = jnp.exp(v2 - v1)
    den = 1.0 + e21
    w1 = (1.0 / den) * grp_w
    w2 = (e21 / den) * grp_w
    return jnp.where(lane == i1, w1, 0.0) + jnp.where(lane == i2, w2, 0.0), grp


def _moe_kernel(x1_ref, h2_ref, lg_ref, ltri_ref, wgu_ref, wd_ref, gf_ref, o_ref, xs_ref, cs_ref, ys_ref, *,
                final_norm):
    tm = h2_ref.shape[0]
    nblk = tm // MOE_BLOCK
    comb, grp = _combine_weights(lg_ref[...])
    lane = lax.broadcasted_iota(jnp.int32, comb.shape, 1)
    lane1 = lax.broadcasted_iota(jnp.int32, (1, ROUTER_COLS), 1)
    onehot = lane == grp
    cnt = jnp.sum(jnp.where(onehot, 1.0, 0.0), axis=0, keepdims=True)
    offs = [jnp.sum(jnp.where(lane1 < g, cnt, 0.0)) for g in range(N_GROUPS)]
    off_row = sum(jnp.where(lane1 == g, offs[g], 0.0) for g in range(N_GROUPS))
    rank = jnp.dot(ltri_ref[...], jnp.where(onehot, 1.0, 0.0).astype(BF16), preferred_element_type=F32)
    pos = jnp.sum(jnp.where(onehot, rank + off_row, 0.0), axis=1, keepdims=True).astype(jnp.int32)
    pos_row = jnp.where(lane == 0, pos, 0).astype(F32).T[0:1, :].astype(jnp.int32)
    perm = jnp.where(lax.broadcasted_iota(jnp.int32, (tm, tm), 0) == pos_row, 1.0, 0.0).astype(BF16)
    perm_t = jnp.where(lax.broadcasted_iota(jnp.int32, (tm, tm), 1) == pos, 1.0, 0.0).astype(BF16)

    c_hi = comb.astype(BF16)
    c_lo = (comb - c_hi.astype(F32)).astype(BF16)
    srt = jnp.dot(perm, jnp.concatenate([h2_ref[...], c_hi, c_lo], axis=1), preferred_element_type=F32)
    d = h2_ref.shape[1]
    xs_ref[...] = srt[:, 0:d].astype(BF16)
    cs_ref[...] = srt[:, d:d + ROUTER_COLS] + srt[:, d + ROUTER_COLS:d + 2 * ROUTER_COLS]
    ys_ref[...] = jnp.zeros(ys_ref.shape, F32)

    starts = [o.astype(jnp.int32) for o in offs]
    passes = []
    for b in range(nblk):
        g_first = sum((starts[g] <= b * MOE_BLOCK).astype(jnp.int32) for g in range(1, N_GROUPS))
        passes.append((jnp.int32(b), g_first, jnp.float32(1.0)))
    for g in range(1, N_GROUPS):
        inside = jnp.logical_and(starts[g] % MOE_BLOCK != 0, starts[g] < tm)
        passes.append((jnp.minimum(starts[g] // MOE_BLOCK, nblk - 1), jnp.int32(g), inside.astype(F32)))

    def gate_up(ps):
        b, g, _ = ps
        xb = xs_ref[pl.ds(pl.multiple_of(b * MOE_BLOCK, MOE_BLOCK), MOE_BLOCK), :]
        return [jnp.dot(xb, wgu_ref[g * EXP_PER_GROUP + j], preferred_element_type=F32)
                for j in range(EXP_PER_GROUP)]

    def finish(ps, gus):
        b, g, live = ps
        rows = pl.ds(pl.multiple_of(b * MOE_BLOCK, MOE_BLOCK), MOE_BLOCK)
        cb = cs_ref[rows, :] * live
        lane_b = lax.broadcasted_iota(jnp.int32, cb.shape, 1)
        acts = []
        for j in range(EXP_PER_GROUP):
            ce = jnp.sum(jnp.where(lane_b == EXPERT_LANE0 + g * EXP_PER_GROUP + j, cb, 0.0), axis=1,
                         keepdims=True)
            gt = gus[j][:, 0:D_EXPERT]
            acts.append((gt * jax.nn.sigmoid(gt) * gus[j][:, D_EXPERT:2 * D_EXPERT] * ce).astype(BF16))
        wrows = pl.ds(pl.multiple_of(g * (EXP_PER_GROUP * D_EXPERT), EXP_PER_GROUP * D_EXPERT),
                      EXP_PER_GROUP * D_EXPERT)
        y = jnp.dot(jnp.concatenate(acts, axis=1), wd_ref[wrows, :], preferred_element_type=F32)
        ys_ref[rows, :] = ys_ref[rows, :] + y

    pending = [gate_up(passes[0])]
    for i, ps in enumerate(passes):
        gus = pending.pop(0)
        if i + 1 < len(passes):
            pending.append(gate_up(passes[i + 1]))
        finish(ps, gus)

    ys = ys_ref[...]
    y_hi = ys.astype(BF16)
    y_lo = (ys - y_hi.astype(F32)).astype(BF16)
    y = x1_ref[...] + (jnp.dot(perm_t, y_hi, preferred_element_type=F32)
                       + jnp.dot(perm_t, y_lo, preferred_element_type=F32))
    if final_norm:
        y = _rms(y, gf_ref[...])
    o_ref[...] = y


def _moe_call(x1, h2, lg, lw, final_g, final_norm):
    t, d = x1.shape
    tm = TM_MOE
    tok = lambda w: pl.BlockSpec((tm, w), lambda i: (i, 0))
    once = pl.Buffered(1)
    idx = jnp.arange(tm)
    ltri = (idx[None, :] < idx[:, None]).astype(BF16)
    return pl.pallas_call(
        functools.partial(_moe_kernel, final_norm=final_norm), grid=(t // tm,),
        in_specs=[tok(d), tok(d), tok(ROUTER_COLS),
                  pl.BlockSpec((tm, tm), lambda i: (0, 0), pipeline_mode=once),
                  pl.BlockSpec((N_EXPERTS, d, 2 * D_EXPERT), lambda i: (0, 0, 0), pipeline_mode=once),
                  pl.BlockSpec((N_EXPERTS * D_EXPERT, d), lambda i: (0, 0), pipeline_mode=once),
                  pl.BlockSpec((1, d), lambda i: (0, 0))],
        out_specs=tok(d),
        out_shape=jax.ShapeDtypeStruct((t, d), F32),
        scratch_shapes=[pltpu.VMEM((tm, d), BF16), pltpu.VMEM((tm, ROUTER_COLS), F32),
                        pltpu.VMEM((tm, d), F32)],
        compiler_params=pltpu.CompilerParams(dimension_semantics=("parallel",),
                                             vmem_limit_bytes=VMEM_LIMIT),
        name="moe",
    )(x1, h2, lg, ltri, lw["w_gu"], lw["w_dn"], final_g)


def _layer_weights(l, w_in, ln1_g, sink_a, q_norm_g, w_uq, kv_norm_g, w_ukv, q_norm_c, k_norm_c,
                   w_br_a, w_br_b, w_br_c, w_out, ln2_g, w_grp, b_grp, w_exr, b_exr, w_gate, w_up, w_down):
    wi = w_in[l]
    d = wi.shape[0]
    z = lambda n: jnp.zeros((d, n), F32)
    w_qt = jnp.concatenate([wi[:, 768:1024], wi[:, 1024:1152], wi[:, 1184:1696], wi[:, 0:512],
                            wi[:, 640:768], wi[:, 1824:1952]], axis=1).T.astype(BF16)
    w_k = jnp.concatenate([wi[:, 512:640], wi[:, 1024:1152], z(64), wi[:, 1152:1184], z(32),
                           wi[:, 1696:1760], z(64), wi[:, 1760:1824], z(64)], axis=1).astype(BF16)
    wq = w_uq[l].reshape(Q_LORA, HB, NOPE_B + ROPE_B)
    wq = jnp.concatenate([wq, jnp.zeros((Q_LORA, HB, HEAD_PAD - NOPE_B - ROPE_B), F32)], axis=-1)
    wkv = w_ukv[l].reshape(KV_LORA, HB, NOPE_B + V_B)
    wk = jnp.concatenate([wkv[..., :NOPE_B], jnp.zeros((KV_LORA, HB, HEAD_PAD - NOPE_B), F32)], axis=-1)
    zero64 = jnp.zeros((HD_C,), F32)
    w_router = jnp.concatenate([w_grp[l], w_exr[l], jnp.zeros((d, ROUTER_COLS - N_GROUPS - N_EXPERTS), F32)], 1)
    b_router = jnp.concatenate([b_grp[l], b_exr[l], jnp.zeros((ROUTER_COLS - N_GROUPS - N_EXPERTS,), F32)])
    return dict(
        ln1_g=ln1_g[l].reshape(1, d), w_qt=w_qt, w_k=w_k, w_gate3=wi[:, 1952:].astype(BF16),
        sink=jnp.repeat(sink_a[l] * LOG2E, TQ_WINDOW).reshape(HA_Q // 2, 1, 2 * TQ_WINDOW),
        q_norm_col=q_norm_g[l].reshape(Q_LORA, 1),
        w_uqt=wq.reshape(Q_LORA, HB * HEAD_PAD).T.astype(BF16),
        kv_norm_g=kv_norm_g[l].reshape(1, KV_LORA), kv_norm_col=kv_norm_g[l].reshape(KV_LORA, 1),
        w_uk=wk.reshape(KV_LORA, HB * HEAD_PAD).astype(BF16),
        w_uvt=wkv[..., NOPE_B:].reshape(KV_LORA, HB * V_B).T.astype(BF16),
        gqc_col=jnp.tile(q_norm_c[l], HC_Q).reshape(512, 1),
        gkc=jnp.concatenate([k_norm_c[l], zero64, k_norm_c[l], zero64]).reshape(1, 256),
        w_br_a=w_br_a[l].astype(BF16), w_br_b=w_br_b[l].astype(BF16), w_br_c=w_br_c[l].astype(BF16),
        w_out=w_out[l].astype(BF16), ln2_g=ln2_g[l].reshape(1, d),
        w_router=w_router.astype(BF16), b_router=b_router.reshape(1, ROUTER_COLS),
        w_gu=jnp.concatenate([w_gate[l], w_up[l]], axis=-1).astype(BF16),
        w_dn=w_down[l].reshape(N_EXPERTS * D_EXPERT, d).astype(BF16),
    )


def _t5_bucket(rel):
    nb = N_BUCKETS // 2
    bucket = jnp.where(rel > 0, nb, 0)
    n = jnp.abs(rel)
    max_exact = nb // 2
    nf = jnp.maximum(n, 1).astype(F32)
    large = max_exact + (jnp.log(nf / max_exact) / math.log(MAX_DISTANCE / max_exact)
                         * (nb - max_exact)).astype(jnp.int32)
    large = jnp.minimum(large, nb - 1)
    return bucket + jnp.where(n < max_exact, n, large)


def _tables(s, rel_bias):
    half = ROPE_B // 2
    inv = ROPE_THETA ** (-jnp.arange(half, dtype=F32) / half)
    pos = jnp.arange(s, dtype=jnp.int32)
    row = (pos // GRID_W).astype(F32)
    col = (pos % GRID_W).astype(F32)

    def cs(p):
        ang = p[:, None] * inv[None, :]
        return jnp.cos(ang), jnp.sin(ang)

    cp, sp = cs(pos.astype(F32))
    cr, sr = cs(row)
    cc, sc = cs(col)
    one = jnp.ones((s, 64), F32)
    z32 = jnp.zeros((s, 32), F32)
    z64 = jnp.zeros((s, 64), F32)
    b_cos = jnp.concatenate([one, cp, cp, z32], axis=1)
    b_sin = jnp.concatenate([z64, -sp, sp, z32], axis=1)
    c_cos = jnp.concatenate([cr, cr, cc, cc], axis=1)
    c_sin = jnp.concatenate([-sr, sr, -sc, sc], axis=1)
    scale_b = (NOPE_B + ROPE_B) ** -0.5 * LOG2E
    scale_c = HD_C ** -0.5 * LOG2E
    gidx = jnp.arange(256) // HD_C
    gmat = (gidx[:, None] == gidx[None, :]).astype(BF16)
    period = 4 * TQ_WINDOW + 1
    mats = []
    for shift in (0, WINDOW, 2 * WINDOW):
        rel = jnp.arange(period) - TQ_WINDOW - shift
        prof = jnp.where((jnp.abs(rel) <= WINDOW)[:, None], rel_bias[_t5_bucket(rel)].astype(F32) * LOG2E,
                         NEG_INF).T
        skew = jnp.tile(prof, (1, TQ_WINDOW))[:, :TQ_WINDOW * (period - 1)]
        skew = skew.reshape(HA_Q, TQ_WINDOW, period - 1)[:, :, TQ_WINDOW:TQ_WINDOW + TK_WINDOW]
        bias = jnp.transpose(skew, (0, 2, 1)).reshape(HA_Q // 2, 2, TK_WINDOW, TQ_WINDOW)
        mats.append(jnp.transpose(bias, (0, 2, 1, 3)).reshape(HA_Q // 2, TK_WINDOW, 2 * TQ_WINDOW))
    return dict(
        bqct=(b_cos * scale_b).T, bqst=(b_sin * scale_b).T, bkc=b_cos, bks=b_sin,
        cqct=(jnp.concatenate([c_cos, c_cos], 1) * scale_c).T,
        cqst=(jnp.concatenate([c_sin, c_sin], 1) * scale_c).T,
        ckc=jnp.concatenate([c_cos, z64], 1), cks=jnp.concatenate([c_sin, z64], 1),
        gmat=gmat, bias3=jnp.stack(mats, axis=0), **_key_norm_constants())


def _key_norm_constants():
    lb = jnp.arange(HB * HEAD_PAD)
    ghb = ((lb[:, None] // HEAD_PAD == jnp.arange(LANE)[None, :]) & (lb[:, None] % HEAD_PAD < ONES_LANE_B))
    lc = jnp.arange(HC_KV * LANE)
    ghc = ((lc[:, None] // LANE == jnp.arange(LANE)[None, :]) & (lc[:, None] % LANE < ONES_LANE_C))
    return dict(ghb=ghb.astype(BF16), ghc=ghc.astype(BF16),
                oneb=(lb % HEAD_PAD == ONES_LANE_B).astype(F32).reshape(1, -1),
                onec=(lc % LANE == ONES_LANE_C).astype(F32).reshape(1, -1))


def _trunk(x, layers, tabs, final_g):
    b, s, d = x.shape
    n_layers = len(layers)
    for l, lw in enumerate(layers):
        qa, ka, va, qbt, kb, vbt, qct, kc, vct, kbn, kcn = _prep_call(x, lw, tabs)
        oa = _window_call(qa, ka, va, lw["sink"], tabs["bias3"])
        ob = _dense_call(qbt, kb, vbt, kbn, n_heads=HB, heads_per_kv=1, dq=HEAD_PAD, dk=HEAD_PAD,
                         k_stride=HEAD_PAD, ones_lane=ONES_LANE_B, name="latent_attn")
        oc = _dense_call(qct, kc, vct, kcn, n_heads=HC_Q, heads_per_kv=HC_Q // HC_KV, dq=HD_C, dk=HD_C,
                         k_stride=LANE, ones_lane=ONES_LANE_C, name="axial_attn")
        t = b * s
        x1, h2, lg = _merge_call(x.reshape(t, d), oa.reshape(t, 512), ob.reshape(t, 512),
                                 oc.reshape(t, 512), lw)
        x = _moe_call(x1, h2, lg, lw, final_g, l == n_layers - 1).reshape(b, s, d)
    return x


def kernel(x_prompt, x_sample, rel_bias, final_g, ln1_g, w_in, sink_a, q_norm_g, w_uq, kv_norm_g, w_ukv,
           q_norm_c, k_norm_c, w_br_a, w_br_b, w_br_c, w_out, ln2_g, w_grp, b_grp, w_exr, b_exr, w_gate,
           w_up, w_down):
    n_layers = w_in.shape[0]
    layers = [_layer_weights(l, w_in, ln1_g, sink_a, q_norm_g, w_uq, kv_norm_g, w_ukv, q_norm_c, k_norm_c,
                             w_br_a, w_br_b, w_br_c, w_out, ln2_g, w_grp, b_grp, w_exr, b_exr, w_gate,
                             w_up, w_down) for l in range(n_layers)]
    fg = final_g.reshape(1, -1)
    outs = []
    for x in (x_prompt, x_sample):
        tabs = _tables(x.shape[1], rel_bias)
        outs.append(_trunk(x, layers, tabs, fg))
    return tuple(outs)
```
